```python
import math
import jax
import jax.numpy as jnp
from jax import lax
import numpy as np

D_MODEL = 1024
BATCH = 8
SEQ = 4096
DEPTH = 2

N_MIXERS = 2
HEAD_DIM = 64
MIX_WIDTH = 3 * D_MODEL // 4
MEM_WIDTH = D_MODEL - MIX_WIDTH
MEM_HEADS = 4
MEM_HEAD_DIM = MEM_WIDTH // MEM_HEADS
MEM_TOKENS = 256
RWKV_HEADS = MIX_WIDTH // HEAD_DIM
DECAY_LORA = 64
ICLR_LORA = 64
GATE_LORA = 128
RWKV_COLS = 3 * MIX_WIDTH + DECAY_LORA + ICLR_LORA + GATE_LORA
RWKV_IN = RWKV_COLS + MEM_WIDTH
DIFF_HEADS = MIX_WIDTH // (2 * HEAD_DIM)
DIFF_IN = 3 * MIX_WIDTH + MEM_WIDTH
D_FF = 4 * D_MODEL
N_RWKV_LAYERS = (DEPTH + N_MIXERS - 1) // N_MIXERS
N_DIFF_LAYERS = DEPTH // N_MIXERS
Q_BLOCK = 128
RMS_EPS = 1e-6
GN_EPS = 64e-5

kernel_name = "rwkv7_diffattn_alibi_memxattn_hybrid"


def rms_norm(x, g=None):
    xf = x.astype(jnp.float32)
    y = xf * lax.rsqrt(jnp.mean(jnp.square(xf), axis=-1, keepdims=True) + RMS_EPS)
    if g is not None:
        y = y * g.astype(jnp.float32)
    return y.astype(x.dtype)


def token_shift(z, mu):
    z_prev = jnp.pad(z[:, :-1], ((0, 0), (1, 0), (0, 0)))
    return z + (z_prev - z) * mu


def delta_rule_scan(r, w, k, v, a, b):
    B, T, H, N = r.shape
    xs = tuple(jnp.moveaxis(z.astype(jnp.float32), 1, 0) for z in (r, w, k, v, a, b))

    def step(S, inp):
        r_t, w_t, k_t, v_t, a_t, b_t = inp
        sa = jnp.einsum('bhvk,bhk->bhv', S, a_t)
        S = (S * w_t[:, :, None, :] + sa[..., None] * b_t[:, :, None, :]
             + v_t[..., None] * k_t[:, :, None, :])
        return S, jnp.einsum('bhvk,bhk->bhv', S, r_t)

    S0 = jnp.zeros((B, H, N, N), jnp.float32)
    _, y = lax.scan(step, S0, xs)
    return jnp.moveaxis(y, 0, 1)


def rwkv7_time_mix(h, w_in, mu, w0, w2, a0, a2, g2, k_k, k_a, r_k, lnx_g, lnx_b):
    B, T, _ = h.shape
    H, N = RWKV_HEADS, HEAD_DIM
    proj = h @ w_in
    slab = token_shift(proj[..., :RWKV_COLS], mu)
    q_mem = proj[..., RWKV_COLS:]
    c3 = 3 * MIX_WIDTH
    r, k, v, wd, ad, gd = jnp.split(
        slab, [MIX_WIDTH, 2 * MIX_WIDTH, c3, c3 + DECAY_LORA, c3 + DECAY_LORA + ICLR_LORA], axis=-1)
    log_w = -jax.nn.softplus(-(w0 + jnp.tanh(wd) @ w2)) - 0.5
    decay = jnp.exp(-jnp.exp(log_w.astype(jnp.float32)))
    a = jax.nn.sigmoid(a0 + ad @ a2)
    g = jax.nn.sigmoid(gd) @ g2

    def heads(z):
        return z.reshape(B, T, H, N)

    kk = heads(k * k_k).astype(jnp.float32)
    kk = kk / jnp.maximum(jnp.sqrt(jnp.sum(jnp.square(kk), axis=-1, keepdims=True)), 1e-12)
    k = k * (1.0 + (a - 1.0) * k_a)
    a_h = heads(a).astype(jnp.float32)
    r_h, k_h, v_h = heads(r), heads(k), heads(v)
    y = delta_rule_scan(r_h, heads(decay), k_h, v_h, -kk, kk * a_h)
    mean = jnp.mean(y, axis=-1, keepdims=True)
    var = jnp.mean(jnp.square(y - mean), axis=-1, keepdims=True)
    y = ((y - mean) * lax.rsqrt(var + GN_EPS)).reshape(B, T, MIX_WIDTH)
    y = (y * lnx_g.astype(jnp.float32) + lnx_b.astype(jnp.float32)).astype(h.dtype)
    bonus = (jnp.sum(r_h * k_h * r_k, axis=-1, keepdims=True) * v_h).reshape(B, T, MIX_WIDTH)
    return (y + bonus) * g, q_mem


def alibi_slopes(n):
    return jnp.exp2(-8.0 * jnp.arange(1, n + 1, dtype=jnp.float32) / n)


def causal_diff_attention(q, k, v, lam, slopes):
    B, T, H, _, d = q.shape
    nb = T // Q_BLOCK
    qb = jnp.moveaxis(q.reshape(B, nb, Q_BLOCK, H, 2, d), 1, 0)
    starts = jnp.arange(nb, dtype=jnp.int32) * Q_BLOCK
    k_pos = jnp.arange(T, dtype=jnp.int32)
    scale = d ** -0.5

    def block(args):
        q_blk, start = args
        q_pos = start + jnp.arange(Q_BLOCK, dtype=jnp.int32)
        dist = q_pos[:, None] - k_pos[None, :]
        bias = -slopes[:, None, None] * dist.astype(jnp.float32)[None]
        s = jnp.einsum('bqhcd,bkhcd->bhcqk', q_blk, k).astype(jnp.float32) * scale
        s = jnp.where(dist >= 0, s + bias[None, :, None], -jnp.inf)
        p = jax.nn.softmax(s, axis=-1)
        p = p[:, :, 0] - lam * p[:, :, 1]
        return jnp.einsum('bhqk,bkhe->bqhe', p.astype(v.dtype), v)

    out = lax.map(block, (qb, starts))
    return jnp.moveaxis(out, 0, 1).reshape(B, T, H, 2 * d)


def diff_lambda_init(layer):
    return 0.8 - 0.6 * math.exp(-0.3 * layer)


def diff_attention_mixer(h, w_in, q_g, k_g, lq1, lk1, lq2, lk2, subln_g, lambda_init):
    B, T, _ = h.shape
    H, d = DIFF_HEADS, HEAD_DIM
    proj = h @ w_in
    q = rms_norm(proj[..., :MIX_WIDTH].reshape(B, T, H, 2, d), q_g)
    k = rms_norm(proj[..., MIX_WIDTH:2 * MIX_WIDTH].reshape(B, T, H, 2, d), k_g)
    v = proj[..., 2 * MIX_WIDTH:3 * MIX_WIDTH].reshape(B, T, H, 2 * d)
    q_mem = proj[..., 3 * MIX_WIDTH:]
    lam = (jnp.exp(jnp.sum(lq1 * lk1).astype(jnp.float32))
           - jnp.exp(jnp.sum(lq2 * lk2).astype(jnp.float32)) + lambda_init)
    o = causal_diff_attention(q, k, v, lam, alibi_slopes(H))
    o = rms_norm(o, subln_g) * (1.0 - lambda_init)
    return o.reshape(B, T, MIX_WIDTH), q_mem


def memory_cross_attention(q_mem, k_mem, v_mem, q_g, k_g):
    B, T, _ = q_mem.shape
    q = rms_norm(q_mem.reshape(B, T, MEM_HEADS, MEM_HEAD_DIM), q_g)
    k = k_mem * k_g
    s = jnp.einsum('bthd,bmhd->bhtm', q, k).astype(jnp.float32) * MEM_HEAD_DIM ** -0.5
    p = jax.nn.softmax(s, axis=-1).astype(v_mem.dtype)
    return jnp.einsum('bhtm,bmhd->bthd', p, v_mem).reshape(B, T, MEM_WIDTH)


def setup_inputs(seed: int = 0) -> dict:
    key = jax.random.key(seed)
    keys = jax.random.split(key, 40)
    counter = [0]

    def next_key():
        counter[0] += 1
        return keys[counter[0] - 1]

    def nrm(shape, scale):
        return jax.random.normal(next_key(), shape, jnp.float32) * scale

    def gain(shape):
        return 1.0 + nrm(shape, 0.05)

    def unif(shape, lo, hi):
        return jax.random.uniform(next_key(), shape, jnp.float32, lo, hi)

    NR, ND = N_RWKV_LAYERS, N_DIFF_LAYERS
    return {
        "x": nrm((BATCH, SEQ, D_MODEL), 1.0),
        "mem": nrm((BATCH, MEM_TOKENS, D_MODEL), 1.0),
        "norm_mix_g": gain((DEPTH, D_MODEL)),
        "norm_ffn_g": gain((DEPTH, D_MODEL)),
        "w_out": nrm((DEPTH, D_MODEL, D_MODEL), D_MODEL ** -0.5),
        "w_ff1": nrm((DEPTH, D_MODEL, D_FF), D_MODEL ** -0.5),
        "w_ff2": nrm((DEPTH, D_FF, D_MODEL), D_FF ** -0.5),
        "mem_norm_g": gain((D_MODEL,)),
        "w_mem_kv": nrm((D_MODEL, 2 * MEM_WIDTH), D_MODEL ** -0.5),
        "mem_q_norm_g": gain((DEPTH, MEM_HEAD_DIM)),
        "mem_k_norm_g": gain((DEPTH, MEM_HEAD_DIM)),
        "rw_in": nrm((NR, D_MODEL, RWKV_IN), D_MODEL ** -0.5),
        "rw_mu": unif((NR, RWKV_COLS), 0.0, 1.0),
        "rw_w0": unif((NR, MIX_WIDTH), -6.0, -1.0),
        "rw_w2": nrm((NR, DECAY_LORA, MIX_WIDTH), 0.5 * DECAY_LORA ** -0.5),
        "rw_a0": nrm((NR, MIX_WIDTH), 0.1),
        "rw_a2": nrm((NR, ICLR_LORA, MIX_WIDTH), ICLR_LORA ** -0.5),
        "rw_g2": nrm((NR, GATE_LORA, MIX_WIDTH), GATE_LORA ** -0.5),
        "rw_k_k": 0.85 + nrm((NR, MIX_WIDTH), 0.05),
        "rw_k_a": gain((NR, MIX_WIDTH)),
        "rw_r_k": nrm((NR, RWKV_HEADS, HEAD_DIM), 0.1),
        "rw_lnx_g": gain((NR, MIX_WIDTH)),
        "rw_lnx_b": nrm((NR, MIX_WIDTH), 0.01),
        "df_in": nrm((ND, D_MODEL, DIFF_IN), D_MODEL ** -0.5),
        "df_q_norm_g": gain((ND, 2, HEAD_DIM)),
        "df_k_norm_g": gain((ND, 2, HEAD_DIM)),
        "df_lq1": nrm((ND, HEAD_DIM), 0.1),
        "df_lk1": nrm((ND, HEAD_DIM), 0.1),
        "df_lq2": nrm((ND, HEAD_DIM), 0.1),
        "df_lk2": nrm((ND, HEAD_DIM), 0.1),
        "df_subln_g": gain((ND, 2 * HEAD_DIM)),
    }


def reference(x, mem, norm_mix_g, norm_ffn_g, w_out, w_ff1, w_ff2, mem_norm_g, w_mem_kv,
              mem_q_norm_g, mem_k_norm_g, rw_in, rw_mu, rw_w0, rw_w2, rw_a0, rw_a2, rw_g2,
              rw_k_k, rw_k_a, rw_r_k, rw_lnx_g, rw_lnx_b, df_in, df_q_norm_g, df_k_norm_g,
              df_lq1, df_lk1, df_lq2, df_lk2, df_subln_g):
    B, M, _ = mem.shape
    kv = rms_norm(mem, mem_norm_g) @ w_mem_kv
    k_mem = rms_norm(kv[..., :MEM_WIDTH].reshape(B, M, MEM_HEADS, MEM_HEAD_DIM))
    v_mem = kv[..., MEM_WIDTH:].reshape(B, M, MEM_HEADS, MEM_HEAD_DIM)

    for layer in range(DEPTH):
        j = layer // N_MIXERS
        h = rms_norm(x, norm_mix_g[layer])
        if layer % N_MIXERS == 0:
            mix, q_mem = rwkv7_time_mix(h, rw_in[j], rw_mu[j], rw_w0[j], rw_w2[j], rw_a0[j],
                                        rw_a2[j], rw_g2[j], rw_k_k[j], rw_k_a[j], rw_r_k[j],
                                        rw_lnx_g[j], rw_lnx_b[j])
        else:
            mix, q_mem = diff_attention_mixer(h, df_in[j], df_q_norm_g[j], df_k_norm_g[j],
                                              df_lq1[j], df_lk1[j], df_lq2[j], df_lk2[j],
                                              df_subln_g[j], diff_lambda_init(layer))
        mem_out = memory_cross_attention(q_mem, k_mem, v_mem, mem_q_norm_g[layer],
                                         mem_k_norm_g[layer])
        x = x + jnp.concatenate([mix, mem_out], axis=-1) @ w_out[layer]
        u = rms_norm(x, norm_ffn_g[layer]) @ w_ff1[layer]
        x = x + jnp.square(jax.nn.relu(u)) @ w_ff2[layer]
    return x
```

```python
import functools
import math

import jax
import jax.numpy as jnp
from jax import lax
from jax.experimental import pallas as pl
from jax.experimental.pallas import tpu as pltpu

F32 = jnp.float32
BF16 = jnp.bfloat16

D_MODEL = 1024
HEAD_DIM = 64
MIX_WIDTH = 768
MEM_WIDTH = 256
MEM_HEADS = 4
RWKV_HEADS = MIX_WIDTH // HEAD_DIM
DECAY_LORA = 64
ICLR_LORA = 64
GATE_LORA = 128
RWKV_COLS = 3 * MIX_WIDTH + DECAY_LORA + ICLR_LORA + GATE_LORA
DIFF_HEADS = MIX_WIDTH // (2 * HEAD_DIM)
D_FF = 4 * D_MODEL
RMS_EPS = 1e-6
GN_EPS = 64e-5

LANES = 128
CHUNK = 64
NEG_BIG = -1e30
VMEM_LIMIT = 56 * 1024 * 1024


def _dot(a, b):
    return jnp.dot(a, b, preferred_element_type=F32)


def _dot_nt(a, b):
    return lax.dot_general(a, b, (((1,), (1,)), ((), ())), preferred_element_type=F32)


def _dot_split(x, m):
    hi = x.astype(BF16)
    lo = (x - hi.astype(F32)).astype(BF16)
    return _dot(hi, m) + _dot(lo, m)


def _dot_split3(m, x):
    hi = x.astype(BF16)
    r1 = x - hi.astype(F32)
    mid = r1.astype(BF16)
    lo = (r1 - mid.astype(F32)).astype(BF16)
    return _dot(m, hi) + _dot(m, mid) + _dot(m, lo)


def _sigmoid(x):
    return 1.0 / (1.0 + jnp.exp(-x))


def _rms_rows(x, g):
    return x * lax.rsqrt(jnp.mean(x * x, axis=-1, keepdims=True) + RMS_EPS) * g


def _group_matrix(n, group, value):
    i = jnp.arange(n) // group
    return jnp.where(i[:, None] == i[None, :], value, 0.0).astype(BF16)


def _params(*sem):
    return pltpu.CompilerParams(dimension_semantics=sem, vmem_limit_bytes=VMEM_LIMIT)


def _const_spec(shape):
    nd = len(shape)
    return pl.BlockSpec(shape, lambda *_: (0,) * nd)


def _mem_kv_kernel(mem_ref, g_ref, w_ref, gm_ref, k_out, v_out):
    h = _rms_rows(mem_ref[0], g_ref[...])
    kv = _dot(h.astype(BF16), w_ref[...])
    k = kv[:, :MEM_WIDTH]
    ms = _dot_split(k * k, gm_ref[...])
    k_out[0] = k * lax.rsqrt(ms + RMS_EPS)
    v_out[0] = kv[:, MEM_WIDTH:]


def _mem_kv(mem, mem_norm_g, w_mem_kv):
    B, M, D = mem.shape
    gm = _group_matrix(MEM_WIDTH, HEAD_DIM, 1.0 / HEAD_DIM)
    out = jax.ShapeDtypeStruct((B, M, MEM_WIDTH), F32)
    return pl.pallas_call(
        _mem_kv_kernel,
        grid=(B,),
        in_specs=[pl.BlockSpec((1, M, D), lambda b: (b, 0, 0)),
                  _const_spec((1, D)), _const_spec((D, 2 * MEM_WIDTH)),
                  _const_spec((MEM_WIDTH, MEM_WIDTH))],
        out_specs=[pl.BlockSpec((1, M, MEM_WIDTH), lambda b: (b, 0, 0))] * 2,
        out_shape=[out, out],
        compiler_params=_params("arbitrary"),
        name="mem_kv",
    )(mem, mem_norm_g.reshape(1, D), w_mem_kv.astype(BF16), gm)


def _rwkv_in_kernel(x_ref, ng_ref, w_ref, mu_ref, w0_ref, w2_ref, a0_ref, a2_ref, g2_ref,
                    kk_ref, ka_ref, rk_ref, gsum_ref, tri_ref,
                    r_out, k_out, v_out, kn_out, a_out, cs_out, bonus_out, g_out, qm_out,
                    carry_ref):
    C = MIX_WIDTH
    tm = x_ref.shape[1]

    @pl.when(pl.program_id(1) == 0)
    def _():
        carry_ref[...] = jnp.zeros_like(carry_ref)

    h = _rms_rows(x_ref[0], ng_ref[...]).astype(BF16)
    proj = _dot(h, w_ref[...])
    cur = proj[:, :RWKV_COLS]
    row = lax.broadcasted_iota(jnp.int32, (tm, 1), 0)
    prev = jnp.where(row == 0, carry_ref[0:1, :], pltpu.roll(cur, 1, axis=0))
    carry_ref[0:1, :] = cur[tm - 1:tm, :]
    slab = cur + (prev - cur) * mu_ref[...]

    r = slab[:, 0:C]
    k = slab[:, C:2 * C]
    v = slab[:, 2 * C:3 * C]
    wa = slab[:, 3 * C:3 * C + DECAY_LORA + ICLR_LORA]
    gd = slab[:, 3 * C + DECAY_LORA + ICLR_LORA:RWKV_COLS]

    z = -(w0_ref[...] + _dot(jnp.tanh(wa).astype(BF16), w2_ref[...]))
    softplus = jnp.maximum(z, 0.0) + jnp.log(1.0 + jnp.exp(-jnp.abs(z)))
    log_decay = -jnp.exp(-softplus - 0.5)
    a = _sigmoid(a0_ref[...] + _dot(wa.astype(BF16), a2_ref[...]))
    g = _dot(_sigmoid(gd).astype(BF16), g2_ref[...])

    gsum = gsum_ref[...]

    def head_sum(t):
        return jnp.concatenate(
            [_dot_split(t[:, i * 256:(i + 1) * 256], gsum) for i in range(C // 256)], axis=1)

    kn = k * kk_ref[...]
    kn = kn / jnp.maximum(jnp.sqrt(head_sum(kn * kn)), 1e-12)
    k2 = k * (1.0 + (a - 1.0) * ka_ref[...])
    bonus = head_sum(r * k2 * rk_ref[...]) * v

    r_out[0] = r
    k_out[0] = k2
    v_out[0] = v
    kn_out[0] = kn
    a_out[0] = a
    cs_out[0] = _dot_split3(tri_ref[...], log_decay)
    bonus_out[0] = bonus
    g_out[0] = g
    qm_out[0] = proj[:, RWKV_COLS:]


def _rwkv_in(x, norm_g, w_in, mu, w0, w2, a0, a2, g2, k_k, k_a, r_k, tm):
    B, T, D = x.shape
    C = MIX_WIDTH
    n_in = w_in.shape[1]
    zeros = jnp.zeros((DECAY_LORA, C), F32)
    w2p = jnp.concatenate([w2, zeros], axis=0).astype(BF16)
    a2p = jnp.concatenate([zeros, a2], axis=0).astype(BF16)
    gsum = _group_matrix(256, HEAD_DIM, 1.0)
    t_idx = jnp.arange(tm)
    tri = ((t_idx[:, None] // CHUNK == t_idx[None, :] // CHUNK)
           & (t_idx[None, :] <= t_idx[:, None])).astype(BF16)
    row = lambda p: p.reshape(1, -1)
    big = jax.ShapeDtypeStruct((B, T, C), F32)
    tile = pl.BlockSpec((1, tm, C), lambda b, t: (b, t, 0))
    return pl.pallas_call(
        _rwkv_in_kernel,
        grid=(B, T // tm),
        in_specs=[pl.BlockSpec((1, tm, D), lambda b, t: (b, t, 0)),
                  _const_spec((1, D)), _const_spec((D, n_in)), _const_spec((1, RWKV_COLS)),
                  _const_spec((1, C)), _const_spec((2 * DECAY_LORA, C)),
                  _const_spec((1, C)), _const_spec((2 * ICLR_LORA, C)),
                  _const_spec((GATE_LORA, C)),
                  _const_spec((1, C)), _const_spec((1, C)), _const_spec((1, C)),
                  _const_spec((256, 256)), _const_spec((tm, tm))],
        out_specs=[tile] * 8 + [pl.BlockSpec((1, tm, MEM_WIDTH), lambda b, t: (b, t, 0))],
        out_shape=[big] * 8 + [jax.ShapeDtypeStruct((B, T, MEM_WIDTH), F32)],
        scratch_shapes=[pltpu.VMEM((8, RWKV_COLS), F32)],
        compiler_params=_params("arbitrary", "arbitrary"),
        name="rwkv_in",
    )(x, row(norm_g), w_in.astype(BF16), row(mu), row(w0), w2p, row(a0), a2p, g2.astype(BF16),
      row(k_k), row(k_a), row(r_k), gsum, tri)


def _rwkv_scan_kernel(r_ref, k_ref, v_ref, kn_ref, a_ref, cs_ref, bonus_ref, g_ref,
                      lng_ref, lnb_ref, gmean_ref, out_ref, s_ref, y_ref):
    L = CHUNK
    n_chunks = r_ref.shape[1] // L

    @pl.when(pl.program_id(2) == 0)
    def _():
        s_ref[...] = jnp.zeros_like(s_ref)

    lane = lax.broadcasted_iota(jnp.int32, (L, LANES), 1)
    row = lax.broadcasted_iota(jnp.int32, (L, LANES), 0)
    head0 = lane < HEAD_DIM
    src = lane & (HEAD_DIM - 1)
    strict = row > src
    incl = row >= src
    r2 = lax.broadcasted_iota(jnp.int32, (LANES, LANES), 0)
    c2 = lax.broadcasted_iota(jnp.int32, (LANES, LANES), 1)
    same_head = (r2 >> 6) == (c2 >> 6)
    eye = r2 == c2
    eye_f = jnp.where(eye, 1.0, 0.0)

    def stack2(t):
        return jnp.concatenate([jnp.where(head0, t, 0.0), jnp.where(head0, 0.0, t)], axis=0)

    def chunk(c, carry):
        rows = pl.ds(pl.multiple_of(c * L, L), L)
        r = r_ref[0, rows, :]
        k = k_ref[0, rows, :]
        v = v_ref[0, rows, :]
        kn = kn_ref[0, rows, :]
        a = a_ref[0, rows, :]
        cs = cs_ref[0, rows, :]
        cs_prev = jnp.where(row == 0, 0.0, pltpu.roll(cs, 1, axis=0))
        w_t = jnp.exp(cs)
        w_inv = jnp.exp(-cs)
        w_prev = jnp.exp(cs_prev)
        w_last = w_t[L - 1:L, :]
        rt = r * w_t
        kt = k * w_inv
        at = -kn * w_prev
        bt = kn * a * w_inv

        lhs = jnp.concatenate([at, rt], axis=0).astype(BF16)
        rhs = jnp.concatenate([stack2(bt), stack2(kt)], axis=0).astype(BF16)
        sc = _dot_nt(lhs, rhs)
        a_ab = jnp.where(strict, sc[:L, :LANES], 0.0)
        a_ak = jnp.where(strict, sc[:L, LANES:], 0.0)
        a_rb = jnp.where(incl, sc[L:, :LANES], 0.0)
        a_rk = jnp.where(incl, sc[L:, LANES:], 0.0)

        npow = stack2(a_ab)
        inv = eye_f + npow
        for _ in range(int(math.log2(L)) - 1):
            nb = npow.astype(BF16)
            npow = _dot(nb, nb)
            inv = inv + _dot(inv.astype(BF16), npow.astype(BF16))
        t_pk = (inv[:L] + inv[L:]).astype(BF16)

        v2 = stack2(v).astype(BF16)
        q0 = _dot(a_ak.astype(BF16), v2)
        pq = _dot(t_pk, jnp.concatenate([stack2(at), stack2(q0)], axis=1).astype(BF16))
        p = pq[:, :LANES]
        q = pq[:, LANES:]
        ar = _dot(a_rb.astype(BF16), jnp.concatenate([stack2(p), stack2(q)], axis=1).astype(BF16))
        r_hat = rt + ar[:, :LANES]
        y0 = ar[:, LANES:] + _dot(a_rk.astype(BF16), v2)

        bk = jnp.concatenate([bt * w_last, kt * w_last], axis=0)
        pv = jnp.concatenate([jnp.concatenate([p, q], axis=1),
                              jnp.concatenate([jnp.zeros_like(v), v], axis=1)], axis=0)
        mc = _dot(bk.T.astype(BF16), pv.astype(BF16))
        m_bd = jnp.where(same_head, mc[:, :LANES], 0.0) + jnp.where(eye, w_last, 0.0)
        c_bd = jnp.where(same_head, mc[:, LANES:], 0.0)

        sb = s_ref[...].astype(BF16)
        y_ref[rows, :] = _dot(r_hat.astype(BF16), sb) + y0
        s_ref[...] = _dot(m_bd.astype(BF16), sb) + c_bd
        return carry

    lax.fori_loop(0, n_chunks, chunk, 0, unroll=2)

    y = y_ref[...]
    gmean = gmean_ref[...]
    d = y - _dot_split(y, gmean)
    var = _dot_split(d * d, gmean)
    y = d * lax.rsqrt(var + GN_EPS) * lng_ref[...] + lnb_ref[...]
    out_ref[0] = (y + bonus_ref[0]) * g_ref[0]


def _rwkv_scan(r, k, v, kn, a, cs, bonus, g, lnx_g, lnx_b, tc):
    B, T, C = r.shape
    gmean = _group_matrix(LANES, HEAD_DIM, 1.0 / HEAD_DIM)
    tile = pl.BlockSpec((1, tc, LANES), lambda b, h, t: (b, t, h))
    vec = pl.BlockSpec((1, LANES), lambda b, h, t: (0, h))
    return pl.pallas_call(
        _rwkv_scan_kernel,
        grid=(B, C // LANES, T // tc),
        in_specs=[tile] * 8 + [vec, vec, _const_spec((LANES, LANES))],
        out_specs=tile,
        out_shape=jax.ShapeDtypeStruct((B, T, C), F32),
        scratch_shapes=[pltpu.VMEM((LANES, LANES), F32), pltpu.VMEM((tc, LANES), F32)],
        compiler_params=_params("arbitrary", "arbitrary", "arbitrary"),
        name="rwkv_scan",
    )(r, k, v, kn, a, cs, bonus, g, lnx_g.reshape(1, C), lnx_b.reshape(1, C), gmean)


def _diff_in_kernel(x_ref, ng_ref, w_ref, qg_ref, kg_ref, gmean_ref, q_out, k_out, v_out, qm_out):
    C = MIX_WIDTH
    h = _rms_rows(x_ref[0], ng_ref[...]).astype(BF16)
    proj = _dot(h, w_ref[...])
    gmean = gmean_ref[...]

    def head_norm(t):
        ms = jnp.concatenate(
            [_dot_split(jnp.square(t[:, i * 256:(i + 1) * 256]), gmean) for i in range(C // 256)], axis=1)
        return t * lax.rsqrt(ms + RMS_EPS)

    q_out[0] = (head_norm(proj[:, :C]) * qg_ref[...]).astype(BF16)
    k_out[0] = (head_norm(proj[:, C:2 * C]) * kg_ref[...]).astype(BF16)
    v_out[0] = proj[:, 2 * C:3 * C].astype(BF16)
    qm_out[0] = proj[:, 3 * C:]


def _diff_in(x, norm_g, w_in, q_g, k_g, tm):
    B, T, D = x.shape
    C = MIX_WIDTH
    n_in = w_in.shape[1]
    gmean = _group_matrix(256, HEAD_DIM, 1.0 / HEAD_DIM)
    qg = jnp.tile(q_g.reshape(-1), DIFF_HEADS).reshape(1, C) * HEAD_DIM ** -0.5
    kg = jnp.tile(k_g.reshape(-1), DIFF_HEADS).reshape(1, C)
    tile = pl.BlockSpec((1, tm, C), lambda b, t: (b, t, 0))
    big = jax.ShapeDtypeStruct((B, T, C), BF16)
    return pl.pallas_call(
        _diff_in_kernel,
        grid=(B, T // tm),
        in_specs=[pl.BlockSpec((1, tm, D), lambda b, t: (b, t, 0)),
                  _const_spec((1, D)), _const_spec((D, n_in)),
                  _const_spec((1, C)), _const_spec((1, C)), _const_spec((256, 256))],
        out_specs=[tile] * 3 + [pl.BlockSpec((1, tm, MEM_WIDTH), lambda b, t: (b, t, 0))],
        out_shape=[big] * 3 + [jax.ShapeDtypeStruct((B, T, MEM_WIDTH), F32)],
        compiler_params=_params("arbitrary", "arbitrary"),
        name="diff_in",
    )(x, norm_g.reshape(1, D), w_in.astype(BF16), qg, kg, gmean)


def _diff_flash_kernel(slopes_ref, q_ref, k_ref, v_ref, lq1_ref, lk1_ref, lq2_ref, lk2_ref, sg_ref,
                       out_ref, *, lambda_init, tk):
    tq = q_ref.shape[1]
    head = pl.program_id(1)
    qi = pl.program_id(2)
    slope = slopes_ref[head]
    q = q_ref[0]
    lane = lax.broadcasted_iota(jnp.int32, (tq, LANES), 1)
    zero = jnp.zeros_like(q)
    qs = jnp.concatenate([jnp.where(lane < HEAD_DIM, q, zero), jnp.where(lane < HEAD_DIM, zero, q)], axis=0)
    col = lax.broadcasted_iota(jnp.int32, (1, tk), 1)
    q_start = qi * tq

    def block(k_start, carry, mask):
        m, l, acc = carry
        kb = k_ref[0, pl.ds(k_start, tk), :]
        vb = v_ref[0, pl.ds(k_start, tk), :]
        s = _dot_nt(qs, kb) + slope * (col + (k_start - q_start)).astype(F32)
        if mask is not None:
            s = jnp.where(mask, s, NEG_BIG)
        m_new = jnp.maximum(m, jnp.max(s, axis=-1, keepdims=True))
        alpha = jnp.exp(m - m_new)
        p = jnp.exp(s - m_new)
        l = alpha * l + jnp.sum(p, axis=-1, keepdims=True)
        acc = alpha * acc + _dot(p.astype(BF16), vb)
        return m_new, l, acc

    carry = (jnp.full((2 * tq, 1), NEG_BIG, F32), jnp.zeros((2 * tq, 1), F32),
             jnp.zeros((2 * tq, LANES), F32))
    carry = lax.fori_loop(
        0, qi * (tq // tk),
        lambda j, c: block(pl.multiple_of(j * tk, tk), c, None), carry)
    q_loc = lax.broadcasted_iota(jnp.int32, (2 * tq, tk), 0) & (tq - 1)
    k_loc = lax.broadcasted_iota(jnp.int32, (2 * tq, tk), 1)
    for d in range(tq // tk):
        carry = block(pl.multiple_of(q_start + d * tk, tk), carry, k_loc + d * tk <= q_loc)
    _, l, acc = carry

    o = acc / l
    lam = (jnp.exp(jnp.sum(lq1_ref[...] * lk1_ref[...], axis=-1, keepdims=True))
           - jnp.exp(jnp.sum(lq2_ref[...] * lk2_ref[...], axis=-1, keepdims=True)) + lambda_init)
    o = o[:tq] - lam * o[tq:]
    out_ref[0] = _rms_rows(o, sg_ref[...]) * (1.0 - lambda_init)


def _diff_flash(q, k, v, lq1, lk1, lq2, lk2, subln_g, lambda_init, tq, tk):
    B, T, C = q.shape
    H = C // LANES
    slopes = jnp.exp2(-8.0 * jnp.arange(1, H + 1, dtype=F32) / H)
    lvec = _const_spec((1, HEAD_DIM))
    seq = pl.BlockSpec((1, T, LANES), lambda b, h, t: (b, 0, h))
    tile = pl.BlockSpec((1, tq, LANES), lambda b, h, t: (b, t, h))
    return pl.pallas_call(
        functools.partial(_diff_flash_kernel, lambda_init=lambda_init, tk=tk),
        grid=(B, H, T // tq),
        in_specs=[pl.BlockSpec(memory_space=pltpu.SMEM), tile, seq, seq,
                  lvec, lvec, lvec, lvec, _const_spec((1, LANES))],
        out_specs=tile,
        out_shape=jax.ShapeDtypeStruct((B, T, C), F32),
        compiler_params=_params("arbitrary", "arbitrary", "arbitrary"),
        name="diff_flash",
    )(slopes, q, k, v, lq1.reshape(1, -1), lk1.reshape(1, -1), lq2.reshape(1, -1), lk2.reshape(1, -1),
      subln_g.reshape(1, LANES))


def _attn_out_kernel(mix_ref, qm_ref, km_ref, vm_ref, x_ref, qg_ref, kg_ref, wa_ref, wb_ref, gmean_ref,
                     out_ref):
    qm = qm_ref[0]
    ms = _dot_split(qm * qm, gmean_ref[...])
    qn = qm * lax.rsqrt(ms + RMS_EPS) * qg_ref[...]
    kb = (km_ref[0] * kg_ref[...]).astype(BF16)
    vb = vm_ref[0].astype(BF16)
    lane = lax.broadcasted_iota(jnp.int32, qn.shape, 1)
    mem = jnp.zeros_like(qn)
    for hd in range(MEM_HEADS):
        in_head = (lane >= hd * HEAD_DIM) & (lane < (hd + 1) * HEAD_DIM)
        s = _dot_nt(jnp.where(in_head, qn, 0.0).astype(BF16), kb)
        p = jnp.exp(s - jnp.max(s, axis=-1, keepdims=True))
        o = _dot(p.astype(BF16), vb) / jnp.sum(p, axis=-1, keepdims=True)
        mem = jnp.where(in_head, o, mem)
    y = _dot(mix_ref[0].astype(BF16), wa_ref[...]) + _dot(mem.astype(BF16), wb_ref[...])
    out_ref[0] = x_ref[0] + y


def _attn_out(mix, qm, k_mem, v_mem, x, q_g, k_g, w_out, tm):
    B, T, D = x.shape
    M = k_mem.shape[1]
    gmean = _group_matrix(MEM_WIDTH, HEAD_DIM, 1.0 / HEAD_DIM)
    qg = jnp.tile(q_g, MEM_HEADS).reshape(1, MEM_WIDTH) * HEAD_DIM ** -0.5
    kg = jnp.tile(k_g, MEM_HEADS).reshape(1, MEM_WIDTH)
    w = w_out.astype(BF16)
    rows = lambda n: pl.BlockSpec((1, tm, n), lambda b, t: (b, t, 0))
    memspec = pl.BlockSpec((1, M, MEM_WIDTH), lambda b, t: (b, 0, 0))
    return pl.pallas_call(
        _attn_out_kernel,
        grid=(B, T // tm),
        in_specs=[rows(MIX_WIDTH), rows(MEM_WIDTH), memspec, memspec, rows(D),
                  _const_spec((1, MEM_WIDTH)), _const_spec((1, MEM_WIDTH)),
                  _const_spec((MIX_WIDTH, D)), _const_spec((MEM_WIDTH, D)),
                  _const_spec((MEM_WIDTH, MEM_WIDTH))],
        out_specs=rows(D),
        out_shape=jax.ShapeDtypeStruct((B, T, D), F32),
        compiler_params=_params("arbitrary", "arbitrary"),
        name="attn_out",
    )(mix, qm, k_mem, v_mem, x, qg, kg, w[:MIX_WIDTH], w[MIX_WIDTH:], gmean)


def _ffn_kernel(x_ref, ng_ref, w1_ref, w2_ref, out_ref, *, ff_chunk):
    x = x_ref[0]
    h = _rms_rows(x, ng_ref[...]).astype(BF16)
    acc = x
    for c in range(w1_ref.shape[1] // ff_chunk):
        u = jnp.maximum(_dot(h, w1_ref[:, c * ff_chunk:(c + 1) * ff_chunk]), 0.0)
        acc = acc + _dot((u * u).astype(BF16), w2_ref[c * ff_chunk:(c + 1) * ff_chunk, :])
    out_ref[0] = acc


def _ffn(x, norm_g, w1, w2, tm):
    B, T, D = x.shape
    F = w1.shape[1]
    rows = pl.BlockSpec((1, tm, D), lambda b, t: (b, t, 0))
    return pl.pallas_call(
        functools.partial(_ffn_kernel, ff_chunk=1024),
        grid=(B, T // tm),
        in_specs=[rows, _const_spec((1, D)), _const_spec((D, F)), _const_spec((F, D))],
        out_specs=rows,
        out_shape=jax.ShapeDtypeStruct((B, T, D), F32),
        compiler_params=_params("arbitrary", "arbitrary"),
        name="ffn",
    )(x, norm_g.reshape(1, D), w1.astype(BF16), w2.astype(BF16))


def _diff_lambda_init(layer):
    return 0.8 - 0.6 * math.exp(-0.3 * layer)


def kernel(x, mem, norm_mix_g, norm_ffn_g, w_out, w_ff1, w_ff2, mem_norm_g, w_mem_kv, mem_q_norm_g, mem_k_norm_g, rw_in, rw_mu, rw_w0, rw_w2, rw_a0, rw_a2, rw_g2, rw_k_k, rw_k_a, rw_r_k, rw_lnx_g, rw_lnx_b, df_in, df_q_norm_g, df_k_norm_g, df_lq1, df_lk1, df_lq2, df_lk2, df_subln_g):
    T = x.shape[1]
    tm = min(T, 256)
    depth = norm_mix_g.shape[0]
    k_mem, v_mem = _mem_kv(mem, mem_norm_g, w_mem_kv)
    for layer in range(depth):
        j = layer // 2
        if layer % 2 == 0:
            r, k, v, kn, a, cs, bonus, g, qm = _rwkv_in(
                x, norm_mix_g[layer], rw_in[j], rw_mu[j], rw_w0[j], rw_w2[j], rw_a0[j], rw_a2[j],
                rw_g2[j], rw_k_k[j], rw_k_a[j], rw_r_k[j], tm)
            mix = _rwkv_scan(r, k, v, kn, a, cs, bonus, g, rw_lnx_g[j], rw_lnx_b[j], min(T, 512))
        else:
            q, k, v, qm = _diff_in(x, norm_mix_g[layer], df_in[j], df_q_norm_g[j], df_k_norm_g[j], tm)
            mix = _diff_flash(q, k, v, df_lq1[j], df_lk1[j], df_lq2[j], df_lk2[j], df_subln_g[j],
                              _diff_lambda_init(layer), min(T, 256), min(T, 256))
        x = _attn_out(mix, qm, k_mem, v_mem, x, mem_q_norm_g[layer], mem_k_norm_g[layer], w_out[layer], tm)
        x = _ffn(x, norm_ffn_g[layer], w_ff1[layer], w_ff2[layer], min(T, 512))
    return x
```

```python
import functools
import math

import jax
import jax.numpy as jnp
from jax import lax
from jax.experimental import pallas as pl
from jax.experimental.pallas import tpu as pltpu

F32 = jnp.float32
BF16 = jnp.bfloat16

D_MODEL = 1024
HEAD_DIM = 64
MIX_WIDTH = 768
MEM_WIDTH = 256
MEM_HEADS = 4
DECAY_LORA = 64
ICLR_LORA = 64
GATE_LORA = 128
RWKV_COLS = 3 * MIX_WIDTH + DECAY_LORA + ICLR_LORA + GATE_LORA
D_FF = 4 * D_MODEL
RMS_EPS = 1e-6
GN_EPS = 64e-5

LANES = 128
HEAD_PAIRS = MIX_WIDTH // LANES
GROUP_COLS = 256
CHUNK = 64
NEG_BIG = -1e30
VMEM_LIMIT = 56 * 1024 * 1024


def _dot(a, b):
    return jnp.dot(a, b, preferred_element_type=F32)


def _dot_nt(a, b):
    return lax.dot_general(a, b, (((1,), (1,)), ((), ())), preferred_element_type=F32)


def _dot_split(x, m):
    hi = x.astype(BF16)
    lo = (x - hi.astype(F32)).astype(BF16)
    return _dot(hi, m) + _dot(lo, m)


def _dot_split3(m, x):
    hi = x.astype(BF16)
    r1 = x - hi.astype(F32)
    mid = r1.astype(BF16)
    lo = (r1 - mid.astype(F32)).astype(BF16)
    return _dot(m, hi) + _dot(m, mid) + _dot(m, lo)


def _sigmoid(x):
    return 1.0 / (1.0 + jnp.exp(-x))


def _rms_rows(x, g):
    return x * lax.rsqrt(jnp.mean(x * x, axis=-1, keepdims=True) + RMS_EPS) * g


def _group_matrix(n, group, value):
    i = jnp.arange(n) // group
    return jnp.where(i[:, None] == i[None, :], value, 0.0).astype(BF16)


def _params(*sem):
    return pltpu.CompilerParams(dimension_semantics=sem, vmem_limit_bytes=VMEM_LIMIT)


def _const_spec(shape):
    nd = len(shape)
    return pl.BlockSpec(shape, lambda *_: (0,) * nd)


def _slab_spec(tm):
    return pl.BlockSpec((1, HEAD_PAIRS, tm, LANES), lambda b, t: (b, 0, t, 0))


def _mem_kv_kernel(mem_ref, g_ref, w_ref, gm_ref, k_out, v_out):
    h = _rms_rows(mem_ref[0], g_ref[...])
    kv = _dot(h.astype(BF16), w_ref[...])
    k = kv[:, :MEM_WIDTH]
    ms = _dot_split(k * k, gm_ref[...])
    k_out[0] = k * lax.rsqrt(ms + RMS_EPS)
    v_out[0] = kv[:, MEM_WIDTH:]


def _mem_kv(mem, mem_norm_g, w_mem_kv):
    B, M, D = mem.shape
    gm = _group_matrix(MEM_WIDTH, HEAD_DIM, 1.0 / HEAD_DIM)
    out = jax.ShapeDtypeStruct((B, M, MEM_WIDTH), F32)
    return pl.pallas_call(
        _mem_kv_kernel,
        grid=(B,),
        in_specs=[pl.BlockSpec((1, M, D), lambda b: (b, 0, 0)),
                  _const_spec((1, D)), _const_spec((D, 2 * MEM_WIDTH)),
                  _const_spec((MEM_WIDTH, MEM_WIDTH))],
        out_specs=[pl.BlockSpec((1, M, MEM_WIDTH), lambda b: (b, 0, 0))] * 2,
        out_shape=[out, out],
        compiler_params=_params("arbitrary"),
        name="mem_kv",
    )(mem, mem_norm_g.reshape(1, D), w_mem_kv.astype(BF16), gm)


def _rwkv_in_kernel(x_ref, ng_ref, w_ref, mu_ref, w0_ref, w2_ref, a0_ref, a2_ref, g2_ref,
                    kk_ref, ka_ref, rk_ref, gsum_ref, tri_ref,
                    r_out, k_out, v_out, kn_out, a_out, cs_out, bonus_out, g_out, qm_out,
                    carry_ref):
    C = MIX_WIDTH
    tm = x_ref.shape[1]

    @pl.when(pl.program_id(1) == 0)
    def _():
        carry_ref[...] = jnp.zeros_like(carry_ref)

    h = _rms_rows(x_ref[0], ng_ref[...]).astype(BF16)
    proj = _dot(h, w_ref[...])
    cur = proj[:, :RWKV_COLS]
    row = lax.broadcasted_iota(jnp.int32, (tm, 1), 0)
    prev = jnp.where(row == 0, carry_ref[0:1, :], pltpu.roll(cur, 1, axis=0))
    carry_ref[0:1, :] = cur[tm - 1:tm, :]
    slab = cur + (prev - cur) * mu_ref[...]

    r = slab[:, 0:C]
    k = slab[:, C:2 * C]
    v = slab[:, 2 * C:3 * C]
    wa = slab[:, 3 * C:3 * C + DECAY_LORA + ICLR_LORA]
    gd = slab[:, 3 * C + DECAY_LORA + ICLR_LORA:RWKV_COLS]

    z = -(w0_ref[...] + _dot(jnp.tanh(wa).astype(BF16), w2_ref[...]))
    softplus = jnp.maximum(z, 0.0) + jnp.log(1.0 + jnp.exp(-jnp.abs(z)))
    log_decay = -jnp.exp(-softplus - 0.5)
    a = _sigmoid(a0_ref[...] + _dot(wa.astype(BF16), a2_ref[...]))
    g = _dot(_sigmoid(gd).astype(BF16), g2_ref[...])

    gsum = gsum_ref[...]

    def head_sum(t):
        return jnp.concatenate(
            [_dot_split(t[:, i:i + GROUP_COLS], gsum) for i in range(0, C, GROUP_COLS)], axis=1)

    kn = k * kk_ref[...]
    kn = kn / jnp.maximum(jnp.sqrt(head_sum(kn * kn)), 1e-12)
    k2 = k * (1.0 + (a - 1.0) * ka_ref[...])
    bonus = head_sum(r * k2 * rk_ref[...]) * v

    cs = _dot_split3(tri_ref[...], log_decay)
    outs = ((r_out, r), (k_out, k2), (v_out, v), (kn_out, kn), (a_out, a), (cs_out, cs),
            (bonus_out, bonus), (g_out, g))
    for ref, val in outs:
        for hp in range(HEAD_PAIRS):
            ref[0, hp] = val[:, hp * LANES:(hp + 1) * LANES]
    qm_out[0] = proj[:, RWKV_COLS:]


def _rwkv_in(x, norm_g, w_in, mu, w0, w2, a0, a2, g2, k_k, k_a, r_k, tm):
    B, T, D = x.shape
    C = MIX_WIDTH
    n_in = w_in.shape[1]
    zeros = jnp.zeros((DECAY_LORA, C), F32)
    w2p = jnp.concatenate([w2, zeros], axis=0).astype(BF16)
    a2p = jnp.concatenate([zeros, a2], axis=0).astype(BF16)
    gsum = _group_matrix(GROUP_COLS, HEAD_DIM, 1.0)
    t_idx = jnp.arange(tm)
    tri = ((t_idx[:, None] // CHUNK == t_idx[None, :] // CHUNK)
           & (t_idx[None, :] <= t_idx[:, None])).astype(BF16)
    row = lambda p: p.reshape(1, -1)
    big = jax.ShapeDtypeStruct((B, HEAD_PAIRS, T, LANES), F32)
    return pl.pallas_call(
        _rwkv_in_kernel,
        grid=(B, T // tm),
        in_specs=[pl.BlockSpec((1, tm, D), lambda b, t: (b, t, 0)),
                  _const_spec((1, D)), _const_spec((D, n_in)), _const_spec((1, RWKV_COLS)),
                  _const_spec((1, C)), _const_spec((2 * DECAY_LORA, C)),
                  _const_spec((1, C)), _const_spec((2 * ICLR_LORA, C)),
                  _const_spec((GATE_LORA, C)),
                  _const_spec((1, C)), _const_spec((1, C)), _const_spec((1, C)),
                  _const_spec((GROUP_COLS, GROUP_COLS)), _const_spec((tm, tm))],
        out_specs=[_slab_spec(tm)] * 8 + [pl.BlockSpec((1, tm, MEM_WIDTH), lambda b, t: (b, t, 0))],
        out_shape=[big] * 8 + [jax.ShapeDtypeStruct((B, T, MEM_WIDTH), F32)],
        scratch_shapes=[pltpu.VMEM((8, RWKV_COLS), F32)],
        compiler_params=_params("arbitrary", "arbitrary"),
        name="rwkv_in",
    )(x, row(norm_g), w_in.astype(BF16), row(mu), row(w0), w2p, row(a0), a2p, g2.astype(BF16),
      row(k_k), row(k_a), row(r_k), gsum, tri)


def _rwkv_scan_kernel(r_ref, k_ref, v_ref, kn_ref, a_ref, cs_ref, bonus_ref, g_ref,
                      lng_ref, lnb_ref, gmean_ref, out_ref,
                      s_ref, rhat_ref, y_ref, m_ref, c_ref):
    L = CHUNK
    tc = r_ref.shape[2]
    n_chunks = tc // L

    @pl.when(pl.program_id(1) == 0)
    def _():
        s_ref[...] = jnp.zeros_like(s_ref)

    lane = lax.broadcasted_iota(jnp.int32, (L, LANES), 1)
    row = lax.broadcasted_iota(jnp.int32, (L, LANES), 0)
    head0 = lane < HEAD_DIM
    src = lane & (HEAD_DIM - 1)
    strict = row > src
    incl = row >= src
    r2 = lax.broadcasted_iota(jnp.int32, (LANES, LANES), 0)
    c2 = lax.broadcasted_iota(jnp.int32, (LANES, LANES), 1)
    same_head = (r2 >> 6) == (c2 >> 6)
    eye = r2 == c2
    eye_f = jnp.where(eye, 1.0, 0.0)
    first_row = (lax.broadcasted_iota(jnp.int32, (tc, LANES), 0) & (L - 1)) == 0

    def stack2(t):
        return jnp.concatenate([jnp.where(head0, t, 0.0), jnp.where(head0, 0.0, t)], axis=0)

    def side2(x, y):
        return jnp.concatenate([stack2(x), stack2(y)], axis=1).astype(BF16)

    def phase_a(hp, carry):
        r = r_ref[0, hp]
        k = k_ref[0, hp]
        v = v_ref[0, hp]
        kn = kn_ref[0, hp]
        a = a_ref[0, hp]
        cs = cs_ref[0, hp]
        cs_prev = jnp.where(first_row, 0.0, pltpu.roll(cs, 1, axis=0))
        w_t = jnp.exp(cs)
        w_inv = jnp.exp(-cs)
        rt_all = r * w_t
        kt_all = k * w_inv
        at_all = -kn * jnp.exp(cs_prev)
        bt_all = kn * a * w_inv

        chunks = range(n_chunks)
        cut = lambda t: [t[c * L:(c + 1) * L] for c in chunks]
        rt, kt, at, bt, vv = cut(rt_all), cut(kt_all), cut(at_all), cut(bt_all), cut(v)
        w_last = [w_t[(c + 1) * L - 1:(c + 1) * L] for c in chunks]

        sc = [_dot_nt(jnp.concatenate([at[c], rt[c]], axis=0).astype(BF16),
                      jnp.concatenate([stack2(bt[c]), stack2(kt[c])], axis=0).astype(BF16))
              for c in chunks]
        a_ab = [jnp.where(strict, s[:L, :LANES], 0.0) for s in sc]
        a_ak = [jnp.where(strict, s[:L, LANES:], 0.0).astype(BF16) for s in sc]
        a_rb = [jnp.where(incl, s[L:, :LANES], 0.0).astype(BF16) for s in sc]
        a_rk = [jnp.where(incl, s[L:, LANES:], 0.0).astype(BF16) for s in sc]

        npow = [stack2(t) for t in a_ab]
        inv = [eye_f + n for n in npow]
        for _ in range(int(math.log2(L)) - 1):
            nb = [n.astype(BF16) for n in npow]
            npow = [_dot(n, n) for n in nb]
            inv = [i + _dot(i.astype(BF16), n.astype(BF16)) for i, n in zip(inv, npow)]
        t_pk = [(i[:L] + i[L:]).astype(BF16) for i in inv]

        v2 = [stack2(t).astype(BF16) for t in vv]
        q0 = [_dot(a_ak[c], v2[c]) for c in chunks]
        pq = [_dot(t_pk[c], side2(at[c], q0[c])) for c in chunks]
        ar = [_dot(a_rb[c], side2(pq[c][:, :LANES], pq[c][:, LANES:])) for c in chunks]
        arkv = [_dot(a_rk[c], v2[c]) for c in chunks]
        mc = [_dot(jnp.concatenate([bt[c] * w_last[c], kt[c] * w_last[c]], axis=0).T.astype(BF16),
                   jnp.concatenate(
                       [pq[c], jnp.concatenate([jnp.zeros_like(vv[c]), vv[c]], axis=1)], axis=0).astype(BF16))
              for c in chunks]
        for c in chunks:
            rows = pl.ds(c * L, L)
            rhat_ref[hp, rows, :] = (rt[c] + ar[c][:, :LANES]).astype(BF16)
            y_ref[hp, rows, :] = ar[c][:, LANES:] + arkv[c]
            m_ref[hp, c] = (jnp.where(same_head, mc[c][:, :LANES], 0.0)
                            + jnp.where(eye, w_last[c], 0.0)).astype(BF16)
            c_ref[hp, c] = jnp.where(same_head, mc[c][:, LANES:], 0.0)
        return carry

    lax.fori_loop(0, HEAD_PAIRS, phase_a, 0)

    def phase_b(c, carry):
        rows = pl.ds(pl.multiple_of(c * L, L), L)
        pairs = range(HEAD_PAIRS)
        sb = [s_ref[hp].astype(BF16) for hp in pairs]
        y = [_dot(rhat_ref[hp, rows, :], sb[hp]) for hp in pairs]
        s_new = [_dot(m_ref[hp, c], sb[hp]) for hp in pairs]
        for hp in pairs:
            y_ref[hp, rows, :] = y_ref[hp, rows, :] + y[hp]
            s_ref[hp] = s_new[hp] + c_ref[hp, c]
        return carry

    lax.fori_loop(0, n_chunks, phase_b, 0)

    gmean = gmean_ref[...]

    def finish(hp, carry):
        y = y_ref[hp]
        d = y - _dot_split(y, gmean)
        var = _dot_split(d * d, gmean)
        y = d * lax.rsqrt(var + GN_EPS) * lng_ref[hp] + lnb_ref[hp]
        out_ref[0, hp] = (y + bonus_ref[0, hp]) * g_ref[0, hp]
        return carry

    lax.fori_loop(0, HEAD_PAIRS, finish, 0)


def _rwkv_scan(r, k, v, kn, a, cs, bonus, g, lnx_g, lnx_b, tc):
    B, HP, T, _ = r.shape
    gmean = _group_matrix(LANES, HEAD_DIM, 1.0 / HEAD_DIM)
    n_chunks = tc // CHUNK
    return pl.pallas_call(
        _rwkv_scan_kernel,
        grid=(B, T // tc),
        in_specs=[_slab_spec(tc)] * 8 + [_const_spec((HP, 1, LANES))] * 2 + [_const_spec((LANES, LANES))],
        out_specs=_slab_spec(tc),
        out_shape=jax.ShapeDtypeStruct((B, HP, T, LANES), F32),
        scratch_shapes=[pltpu.VMEM((HP, LANES, LANES), F32),
                        pltpu.VMEM((HP, tc, LANES), BF16),
                        pltpu.VMEM((HP, tc, LANES), F32),
                        pltpu.VMEM((HP, n_chunks, LANES, LANES), BF16),
                        pltpu.VMEM((HP, n_chunks, LANES, LANES), F32)],
        compiler_params=_params("arbitrary", "arbitrary"),
        name="rwkv_scan",
    )(r, k, v, kn, a, cs, bonus, g, lnx_g.reshape(HP, 1, LANES), lnx_b.reshape(HP, 1, LANES), gmean)


def _diff_in_kernel(x_ref, ng_ref, w_ref, qg_ref, kg_ref, gmean_ref, q_out, k_out, v_out, qm_out):
    C = MIX_WIDTH
    h = _rms_rows(x_ref[0], ng_ref[...]).astype(BF16)
    proj = _dot(h, w_ref[...])
    gmean = gmean_ref[...]

    def head_norm(t):
        ms = jnp.concatenate(
            [_dot_split(jnp.square(t[:, i:i + GROUP_COLS]), gmean) for i in range(0, C, GROUP_COLS)], axis=1)
        return t * lax.rsqrt(ms + RMS_EPS)

    q_out[0] = (head_norm(proj[:, :C]) * qg_ref[...]).astype(BF16)
    k_out[0] = (head_norm(proj[:, C:2 * C]) * kg_ref[...]).astype(BF16)
    v_out[0] = proj[:, 2 * C:3 * C].astype(BF16)
    qm_out[0] = proj[:, 3 * C:]


def _diff_in(x, norm_g, w_in, q_g, k_g, tm):
    B, T, D = x.shape
    C = MIX_WIDTH
    n_in = w_in.shape[1]
    gmean = _group_matrix(GROUP_COLS, HEAD_DIM, 1.0 / HEAD_DIM)
    qg = jnp.tile(q_g.reshape(-1), HEAD_PAIRS).reshape(1, C) * HEAD_DIM ** -0.5
    kg = jnp.tile(k_g.reshape(-1), HEAD_PAIRS).reshape(1, C)
    tile = pl.BlockSpec((1, tm, C), lambda b, t: (b, t, 0))
    big = jax.ShapeDtypeStruct((B, T, C), BF16)
    return pl.pallas_call(
        _diff_in_kernel,
        grid=(B, T // tm),
        in_specs=[pl.BlockSpec((1, tm, D), lambda b, t: (b, t, 0)),
                  _const_spec((1, D)), _const_spec((D, n_in)),
                  _const_spec((1, C)), _const_spec((1, C)), _const_spec((GROUP_COLS, GROUP_COLS))],
        out_specs=[tile] * 3 + [pl.BlockSpec((1, tm, MEM_WIDTH), lambda b, t: (b, t, 0))],
        out_shape=[big] * 3 + [jax.ShapeDtypeStruct((B, T, MEM_WIDTH), F32)],
        compiler_params=_params("arbitrary", "arbitrary"),
        name="diff_in",
    )(x, norm_g.reshape(1, D), w_in.astype(BF16), qg, kg, gmean)


def _diff_flash_kernel(slopes_ref, q_ref, k_ref, v_ref, lq1_ref, lk1_ref, lq2_ref, lk2_ref, sg_ref,
                       out_ref, *, lambda_init, tk):
    tq = q_ref.shape[1]
    head = pl.program_id(1)
    qi = pl.program_id(2)
    slope = slopes_ref[head]
    q = q_ref[0]
    lane = lax.broadcasted_iota(jnp.int32, (tq, LANES), 1)
    zero = jnp.zeros_like(q)
    qs = jnp.concatenate([jnp.where(lane < HEAD_DIM, q, zero), jnp.where(lane < HEAD_DIM, zero, q)], axis=0)
    col = lax.broadcasted_iota(jnp.int32, (1, tk), 1)
    q_start = qi * tq

    def block(k_start, carry, mask):
        m, l, acc = carry
        kb = k_ref[0, pl.ds(k_start, tk), :]
        vb = v_ref[0, pl.ds(k_start, tk), :]
        s = _dot_nt(qs, kb) + slope * (col + (k_start - q_start)).astype(F32)
        if mask is not None:
            s = jnp.where(mask, s, NEG_BIG)
        m_new = jnp.maximum(m, jnp.max(s, axis=-1, keepdims=True))
        alpha = jnp.exp(m - m_new)
        p = jnp.exp(s - m_new)
        l = alpha * l + jnp.sum(p, axis=-1, keepdims=True)
        acc = alpha * acc + _dot(p.astype(BF16), vb)
        return m_new, l, acc

    carry = (jnp.full((2 * tq, 1), NEG_BIG, F32), jnp.zeros((2 * tq, 1), F32),
             jnp.zeros((2 * tq, LANES), F32))
    carry = lax.fori_loop(
        0, qi * (tq // tk),
        lambda j, c: block(pl.multiple_of(j * tk, tk), c, None), carry)
    q_loc = lax.broadcasted_iota(jnp.int32, (2 * tq, tk), 0) & (tq - 1)
    k_loc = lax.broadcasted_iota(jnp.int32, (2 * tq, tk), 1)
    for d in range(tq // tk):
        carry = block(pl.multiple_of(q_start + d * tk, tk), carry, k_loc + d * tk <= q_loc)
    _, l, acc = carry

    o = acc / l
    lam = (jnp.exp(jnp.sum(lq1_ref[...] * lk1_ref[...], axis=-1, keepdims=True))
           - jnp.exp(jnp.sum(lq2_ref[...] * lk2_ref[...], axis=-1, keepdims=True)) + lambda_init)
    o = o[:tq] - lam * o[tq:]
    out_ref[0, 0] = _rms_rows(o, sg_ref[...]) * (1.0 - lambda_init)


def _diff_flash(q, k, v, lq1, lk1, lq2, lk2, subln_g, lambda_init, tq, tk):
    B, T, C = q.shape
    H = C // LANES
    slopes = jnp.exp2(-8.0 * jnp.arange(1, H + 1, dtype=F32) / H)
    lvec = _const_spec((1, HEAD_DIM))
    seq = pl.BlockSpec((1, T, LANES), lambda b, h, t: (b, 0, h))
    return pl.pallas_call(
        functools.partial(_diff_flash_kernel, lambda_init=lambda_init, tk=tk),
        grid=(B, H, T // tq),
        in_specs=[pl.BlockSpec(memory_space=pltpu.SMEM),
                  pl.BlockSpec((1, tq, LANES), lambda b, h, t: (b, t, h)), seq, seq,
                  lvec, lvec, lvec, lvec, _const_spec((1, LANES))],
        out_specs=pl.BlockSpec((1, 1, tq, LANES), lambda b, h, t: (b, h, t, 0)),
        out_shape=jax.ShapeDtypeStruct((B, H, T, LANES), F32),
        compiler_params=_params("arbitrary", "arbitrary", "arbitrary"),
        name="diff_flash",
    )(slopes, q, k, v, lq1.reshape(1, -1), lk1.reshape(1, -1), lq2.reshape(1, -1), lk2.reshape(1, -1),
      subln_g.reshape(1, LANES))


def _attn_out_kernel(mix_ref, qm_ref, km_ref, vm_ref, x_ref, qg_ref, kg_ref, wa_ref, wb_ref, gmean_ref,
                     out_ref):
    qm = qm_ref[0]
    ms = _dot_split(qm * qm, gmean_ref[...])
    qn = qm * lax.rsqrt(ms + RMS_EPS) * qg_ref[...]
    kb = (km_ref[0] * kg_ref[...]).astype(BF16)
    vb = vm_ref[0].astype(BF16)
    lane = lax.broadcasted_iota(jnp.int32, qn.shape, 1)
    mem = jnp.zeros_like(qn)
    for hd in range(MEM_HEADS):
        in_head = (lane >= hd * HEAD_DIM) & (lane < (hd + 1) * HEAD_DIM)
        s = _dot_nt(jnp.where(in_head, qn, 0.0).astype(BF16), kb)
        p = jnp.exp(s - jnp.max(s, axis=-1, keepdims=True))
        o = _dot(p.astype(BF16), vb) / jnp.sum(p, axis=-1, keepdims=True)
        mem = jnp.where(in_head, o, mem)
    mix = jnp.concatenate([mix_ref[0, hp] for hp in range(HEAD_PAIRS)], axis=1)
    y = _dot(mix.astype(BF16), wa_ref[...]) + _dot(mem.astype(BF16), wb_ref[...])
    out_ref[0] = x_ref[0] + y


def _attn_out(mix, qm, k_mem, v_mem, x, q_g, k_g, w_out, tm):
    B, T, D = x.shape
    M = k_mem.shape[1]
    gmean = _group_matrix(MEM_WIDTH, HEAD_DIM, 1.0 / HEAD_DIM)
    qg = jnp.tile(q_g, MEM_HEADS).reshape(1, MEM_WIDTH) * HEAD_DIM ** -0.5
    kg = jnp.tile(k_g, MEM_HEADS).reshape(1, MEM_WIDTH)
    w = w_out.astype(BF16)
    rows = lambda n: pl.BlockSpec((1, tm, n), lambda b, t: (b, t, 0))
    memspec = pl.BlockSpec((1, M, MEM_WIDTH), lambda b, t: (b, 0, 0))
    return pl.pallas_call(
        _attn_out_kernel,
        grid=(B, T // tm),
        in_specs=[_slab_spec(tm), rows(MEM_WIDTH), memspec, memspec, rows(D),
                  _const_spec((1, MEM_WIDTH)), _const_spec((1, MEM_WIDTH)),
                  _const_spec((MIX_WIDTH, D)), _const_spec((MEM_WIDTH, D)),
                  _const_spec((MEM_WIDTH, MEM_WIDTH))],
        out_specs=rows(D),
        out_shape=jax.ShapeDtypeStruct((B, T, D), F32),
        compiler_params=_params("arbitrary", "arbitrary"),
        name="attn_out",
    )(mix, qm, k_mem, v_mem, x, qg, kg, w[:MIX_WIDTH], w[MIX_WIDTH:], gmean)


def _ffn_kernel(x_ref, ng_ref, w1_ref, w2_ref, out_ref, *, ff_chunk):
    x = x_ref[0]
    h = _rms_rows(x, ng_ref[...]).astype(BF16)
    acc = x
    for c in range(0, w1_ref.shape[1], ff_chunk):
        u = jnp.maximum(_dot(h, w1_ref[:, c:c + ff_chunk]), 0.0)
        acc = acc + _dot((u * u).astype(BF16), w2_ref[c:c + ff_chunk, :])
    out_ref[0] = acc


def _ffn(x, norm_g, w1, w2, tm):
    B, T, D = x.shape
    F = w1.shape[1]
    rows = pl.BlockSpec((1, tm, D), lambda b, t: (b, t, 0))
    return pl.pallas_call(
        functools.partial(_ffn_kernel, ff_chunk=D),
        grid=(B, T // tm),
        in_specs=[rows, _const_spec((1, D)), _const_spec((D, F)), _const_spec((F, D))],
        out_specs=rows,
        out_shape=jax.ShapeDtypeStruct((B, T, D), F32),
        compiler_params=_params("arbitrary", "arbitrary"),
        name="ffn",
    )(x, norm_g.reshape(1, D), w1.astype(BF16), w2.astype(BF16))


def _diff_lambda_init(layer):
    return 0.8 - 0.6 * math.exp(-0.3 * layer)


def kernel(x, mem, norm_mix_g, norm_ffn_g, w_out, w_ff1, w_ff2, mem_norm_g, w_mem_kv, mem_q_norm_g, mem_k_norm_g, rw_in, rw_mu, rw_w0, rw_w2, rw_a0, rw_a2, rw_g2, rw_k_k, rw_k_a, rw_r_k, rw_lnx_g, rw_lnx_b, df_in, df_q_norm_g, df_k_norm_g, df_lq1, df_lk1, df_lq2, df_lk2, df_subln_g):
    T = x.shape[1]
    tm = min(T, 256)
    depth = norm_mix_g.shape[0]
    k_mem, v_mem = _mem_kv(mem, mem_norm_g, w_mem_kv)
    for layer in range(depth):
        j = layer // 2
        if layer % 2 == 0:
            r, k, v, kn, a, cs, bonus, g, qm = _rwkv_in(
                x, norm_mix_g[layer], rw_in[j], rw_mu[j], rw_w0[j], rw_w2[j], rw_a0[j], rw_a2[j],
                rw_g2[j], rw_k_k[j], rw_k_a[j], rw_r_k[j], tm)
            mix = _rwkv_scan(r, k, v, kn, a, cs, bonus, g, rw_lnx_g[j], rw_lnx_b[j], min(T, 512))
        else:
            q, k, v, qm = _diff_in(x, norm_mix_g[layer], df_in[j], df_q_norm_g[j], df_k_norm_g[j], tm)
            mix = _diff_flash(q, k, v, df_lq1[j], df_lk1[j], df_lq2[j], df_lk2[j], df_subln_g[j],
                              _diff_lambda_init(layer), min(T, 256), min(T, 256))
        x = _attn_out(mix, qm, k_mem, v_mem, x, mem_q_norm_g[layer], mem_k_norm_g[layer], w_out[layer], tm)
        x = _ffn(x, norm_ffn_g[layer], w_ff1[layer], w_ff2[layer], min(T, 512))
    return x
```

```python
import functools
import math

import jax
import jax.numpy as jnp
from jax import lax
from jax.experimental import pallas as pl
from jax.experimental.pallas import tpu as pltpu

F32 = jnp.float32
BF16 = jnp.bfloat16

D_MODEL = 1024
HEAD_DIM = 64
MIX_WIDTH = 768
MEM_WIDTH = 256
MEM_HEADS = 4
DECAY_LORA = 64
ICLR_LORA = 64
GATE_LORA = 128
RWKV_COLS = 3 * MIX_WIDTH + DECAY_LORA + ICLR_LORA + GATE_LORA
D_FF = 4 * D_MODEL
RMS_EPS = 1e-6
GN_EPS = 64e-5

LANES = 128
HEAD_PAIRS = MIX_WIDTH // LANES
GROUP_COLS = 256
CHUNK = 64
NEG_BIG = -1e30
LOG2_E = math.log2(math.e)
FEAT_RADIX = 32
VMEM_LIMIT = 56 * 1024 * 1024


def _dot(a, b):
    return jnp.dot(a, b, preferred_element_type=F32)


def _dot_nt(a, b):
    return lax.dot_general(a, b, (((1,), (1,)), ((), ())), preferred_element_type=F32)


def _dot_split(x, m):
    hi = x.astype(BF16)
    lo = (x - hi.astype(F32)).astype(BF16)
    return _dot(hi, m) + _dot(lo, m)


def _dot_split3(m, x):
    hi = x.astype(BF16)
    r1 = x - hi.astype(F32)
    mid = r1.astype(BF16)
    lo = (r1 - mid.astype(F32)).astype(BF16)
    return _dot(m, hi) + _dot(m, mid) + _dot(m, lo)


def _sigmoid(x):
    return 1.0 / (1.0 + jnp.exp(-x))


def _rms_rows(x, g):
    return x * lax.rsqrt(jnp.mean(x * x, axis=-1, keepdims=True) + RMS_EPS) * g


def _group_matrix(n, group, value):
    i = jnp.arange(n) // group
    return jnp.where(i[:, None] == i[None, :], value, 0.0).astype(BF16)


def _params(*sem):
    return pltpu.CompilerParams(dimension_semantics=sem, vmem_limit_bytes=VMEM_LIMIT)


def _const_spec(shape):
    nd = len(shape)
    return pl.BlockSpec(shape, lambda *_: (0,) * nd)


def _slab_spec(tm):
    return pl.BlockSpec((1, HEAD_PAIRS, tm, LANES), lambda b, t: (b, 0, t, 0))


def _mem_kv_kernel(mem_ref, g_ref, w_ref, gm_ref, k_out, v_out):
    h = _rms_rows(mem_ref[0], g_ref[...])
    kv = _dot(h.astype(BF16), w_ref[...])
    k = kv[:, :MEM_WIDTH]
    ms = _dot_split(k * k, gm_ref[...])
    k_out[0] = k * lax.rsqrt(ms + RMS_EPS)
    v_out[0] = kv[:, MEM_WIDTH:]


def _mem_kv(mem, mem_norm_g, w_mem_kv):
    B, M, D = mem.shape
    gm = _group_matrix(MEM_WIDTH, HEAD_DIM, 1.0 / HEAD_DIM)
    out = jax.ShapeDtypeStruct((B, M, MEM_WIDTH), F32)
    return pl.pallas_call(
        _mem_kv_kernel,
        grid=(B,),
        in_specs=[pl.BlockSpec((1, M, D), lambda b: (b, 0, 0)),
                  _const_spec((1, D)), _const_spec((D, 2 * MEM_WIDTH)),
                  _const_spec((MEM_WIDTH, MEM_WIDTH))],
        out_specs=[pl.BlockSpec((1, M, MEM_WIDTH), lambda b: (b, 0, 0))] * 2,
        out_shape=[out, out],
        compiler_params=_params("arbitrary"),
        name="mem_kv",
    )(mem, mem_norm_g.reshape(1, D), w_mem_kv.astype(BF16), gm)


def _rwkv_in_kernel(x_ref, ng_ref, w_ref, mu_ref, w0_ref, w2_ref, a0_ref, a2_ref, g2_ref,
                    kk_ref, ka_ref, rk_ref, gsum_ref, tri_ref,
                    r_out, k_out, v_out, kn_out, a_out, cs_out, bonus_out, g_out, qm_out,
                    carry_ref):
    C = MIX_WIDTH
    tm = x_ref.shape[1]

    @pl.when(pl.program_id(1) == 0)
    def _():
        carry_ref[...] = jnp.zeros_like(carry_ref)

    h = _rms_rows(x_ref[0], ng_ref[...]).astype(BF16)
    proj = _dot(h, w_ref[...])
    cur = proj[:, :RWKV_COLS]
    row = lax.broadcasted_iota(jnp.int32, (tm, 1), 0)
    prev = jnp.where(row == 0, carry_ref[0:1, :], pltpu.roll(cur, 1, axis=0))
    carry_ref[0:1, :] = cur[tm - 1:tm, :]
    slab = cur + (prev - cur) * mu_ref[...]

    r = slab[:, 0:C]
    k = slab[:, C:2 * C]
    v = slab[:, 2 * C:3 * C]
    wa = slab[:, 3 * C:3 * C + DECAY_LORA + ICLR_LORA]
    gd = slab[:, 3 * C + DECAY_LORA + ICLR_LORA:RWKV_COLS]

    z = -(w0_ref[...] + _dot(jnp.tanh(wa).astype(BF16), w2_ref[...]))
    softplus = jnp.maximum(z, 0.0) + jnp.log(1.0 + jnp.exp(-jnp.abs(z)))
    log_decay = -jnp.exp(-softplus - 0.5)
    a = _sigmoid(a0_ref[...] + _dot(wa.astype(BF16), a2_ref[...]))
    g = _dot(_sigmoid(gd).astype(BF16), g2_ref[...])

    gsum = gsum_ref[...]

    def head_sum(t):
        return jnp.concatenate(
            [_dot_split(t[:, i:i + GROUP_COLS], gsum) for i in range(0, C, GROUP_COLS)], axis=1)

    kn = k * kk_ref[...]
    kn = kn / jnp.maximum(jnp.sqrt(head_sum(kn * kn)), 1e-12)
    k2 = k * (1.0 + (a - 1.0) * ka_ref[...])
    bonus = head_sum(r * k2 * rk_ref[...]) * v

    cs = _dot_split3(tri_ref[...], log_decay)
    outs = ((r_out, r), (k_out, k2), (v_out, v), (kn_out, kn), (a_out, a), (cs_out, cs),
            (bonus_out, bonus), (g_out, g))
    for ref, val in outs:
        for hp in range(HEAD_PAIRS):
            ref[0, hp] = val[:, hp * LANES:(hp + 1) * LANES]
    qm_out[0] = proj[:, RWKV_COLS:]


def _rwkv_in(x, norm_g, w_in, mu, w0, w2, a0, a2, g2, k_k, k_a, r_k, tm):
    B, T, D = x.shape
    C = MIX_WIDTH
    n_in = w_in.shape[1]
    zeros = jnp.zeros((DECAY_LORA, C), F32)
    w2p = jnp.concatenate([w2, zeros], axis=0).astype(BF16)
    a2p = jnp.concatenate([zeros, a2], axis=0).astype(BF16)
    gsum = _group_matrix(GROUP_COLS, HEAD_DIM, 1.0)
    t_idx = jnp.arange(tm)
    tri = ((t_idx[:, None] // CHUNK == t_idx[None, :] // CHUNK)
           & (t_idx[None, :] <= t_idx[:, None])).astype(BF16)
    row = lambda p: p.reshape(1, -1)
    big = jax.ShapeDtypeStruct((B, HEAD_PAIRS, T, LANES), F32)
    return pl.pallas_call(
        _rwkv_in_kernel,
        grid=(B, T // tm),
        in_specs=[pl.BlockSpec((1, tm, D), lambda b, t: (b, t, 0)),
                  _const_spec((1, D)), _const_spec((D, n_in)), _const_spec((1, RWKV_COLS)),
                  _const_spec((1, C)), _const_spec((2 * DECAY_LORA, C)),
                  _const_spec((1, C)), _const_spec((2 * ICLR_LORA, C)),
                  _const_spec((GATE_LORA, C)),
                  _const_spec((1, C)), _const_spec((1, C)), _const_spec((1, C)),
                  _const_spec((GROUP_COLS, GROUP_COLS)), _const_spec((tm, tm))],
        out_specs=[_slab_spec(tm)] * 8 + [pl.BlockSpec((1, tm, MEM_WIDTH), lambda b, t: (b, t, 0))],
        out_shape=[big] * 8 + [jax.ShapeDtypeStruct((B, T, MEM_WIDTH), F32)],
        scratch_shapes=[pltpu.VMEM((8, RWKV_COLS), F32)],
        compiler_params=_params("arbitrary", "arbitrary"),
        name="rwkv_in",
    )(x, row(norm_g), w_in.astype(BF16), row(mu), row(w0), w2p, row(a0), a2p, g2.astype(BF16),
      row(k_k), row(k_a), row(r_k), gsum, tri)


def _rwkv_scan_kernel(r_ref, k_ref, v_ref, kn_ref, a_ref, cs_ref, bonus_ref, g_ref,
                      lng_ref, lnb_ref, gmean_ref, out_ref,
                      s_ref, rhat_ref, y_ref, m_ref, c_ref):
    L = CHUNK
    tc = r_ref.shape[2]
    n_chunks = tc // L

    @pl.when(pl.program_id(1) == 0)
    def _():
        s_ref[...] = jnp.zeros_like(s_ref)

    lane = lax.broadcasted_iota(jnp.int32, (L, LANES), 1)
    row = lax.broadcasted_iota(jnp.int32, (L, LANES), 0)
    head0 = lane < HEAD_DIM
    src = lane & (HEAD_DIM - 1)
    strict = row > src
    incl = row >= src
    r2 = lax.broadcasted_iota(jnp.int32, (LANES, LANES), 0)
    c2 = lax.broadcasted_iota(jnp.int32, (LANES, LANES), 1)
    same_head = (r2 >> 6) == (c2 >> 6)
    eye = r2 == c2
    eye_f = jnp.where(eye, 1.0, 0.0)
    first_row = (lax.broadcasted_iota(jnp.int32, (tc, LANES), 0) & (L - 1)) == 0

    def stack2(t):
        return jnp.concatenate([jnp.where(head0, t, 0.0), jnp.where(head0, 0.0, t)], axis=0)

    def side2(x, y):
        return jnp.concatenate([stack2(x), stack2(y)], axis=1).astype(BF16)

    def phase_a(hp, carry):
        r = r_ref[0, hp]
        k = k_ref[0, hp]
        v = v_ref[0, hp]
        kn = kn_ref[0, hp]
        a = a_ref[0, hp]
        cs = cs_ref[0, hp]
        cs_prev = jnp.where(first_row, 0.0, pltpu.roll(cs, 1, axis=0))
        w_t = jnp.exp(cs)
        w_inv = jnp.exp(-cs)
        rt_all = r * w_t
        kt_all = k * w_inv
        at_all = -kn * jnp.exp(cs_prev)
        bt_all = kn * a * w_inv

        chunks = range(n_chunks)
        cut = lambda t: [t[c * L:(c + 1) * L] for c in chunks]
        rt, kt, at, bt, vv = cut(rt_all), cut(kt_all), cut(at_all), cut(bt_all), cut(v)
        w_last = [w_t[(c + 1) * L - 1:(c + 1) * L] for c in chunks]

        sc = [_dot_nt(jnp.concatenate([at[c], rt[c]], axis=0).astype(BF16),
                      jnp.concatenate([stack2(bt[c]), stack2(kt[c])], axis=0).astype(BF16))
              for c in chunks]
        a_ab = [jnp.where(strict, s[:L, :LANES], 0.0) for s in sc]
        a_ak = [jnp.where(strict, s[:L, LANES:], 0.0).astype(BF16) for s in sc]
        a_rb = [jnp.where(incl, s[L:, :LANES], 0.0).astype(BF16) for s in sc]
        a_rk = [jnp.where(incl, s[L:, LANES:], 0.0).astype(BF16) for s in sc]

        npow = [stack2(t) for t in a_ab]
        inv = [eye_f + n for n in npow]
        for _ in range(int(math.log2(L)) - 1):
            nb = [n.astype(BF16) for n in npow]
            npow = [_dot(n, n) for n in nb]
            inv = [i + _dot(i.astype(BF16), n.astype(BF16)) for i, n in zip(inv, npow)]
        t_pk = [(i[:L] + i[L:]).astype(BF16) for i in inv]

        v2 = [stack2(t).astype(BF16) for t in vv]
        q0 = [_dot(a_ak[c], v2[c]) for c in chunks]
        pq = [_dot(t_pk[c], side2(at[c], q0[c])) for c in chunks]
        ar = [_dot(a_rb[c], side2(pq[c][:, :LANES], pq[c][:, LANES:])) for c in chunks]
        arkv = [_dot(a_rk[c], v2[c]) for c in chunks]
        mc = [_dot(jnp.concatenate([bt[c] * w_last[c], kt[c] * w_last[c]], axis=0).T.astype(BF16),
                   jnp.concatenate(
                       [pq[c], jnp.concatenate([jnp.zeros_like(vv[c]), vv[c]], axis=1)], axis=0).astype(BF16))
              for c in chunks]
        for c in chunks:
            rows = pl.ds(c * L, L)
            rhat_ref[hp, rows, :] = (rt[c] + ar[c][:, :LANES]).astype(BF16)
            y_ref[hp, rows, :] = ar[c][:, LANES:] + arkv[c]
            m_ref[hp, c] = (jnp.where(same_head, mc[c][:, :LANES], 0.0)
                            + jnp.where(eye, w_last[c], 0.0)).astype(BF16)
            c_ref[hp, c] = jnp.where(same_head, mc[c][:, LANES:], 0.0)
        return carry

    lax.fori_loop(0, HEAD_PAIRS, phase_a, 0)

    def phase_b(c, carry):
        rows = pl.ds(pl.multiple_of(c * L, L), L)
        pairs = range(HEAD_PAIRS)
        sb = [s_ref[hp].astype(BF16) for hp in pairs]
        y = [_dot(rhat_ref[hp, rows, :], sb[hp]) for hp in pairs]
        s_new = [_dot(m_ref[hp, c], sb[hp]) for hp in pairs]
        for hp in pairs:
            y_ref[hp, rows, :] = y_ref[hp, rows, :] + y[hp]
            s_ref[hp] = s_new[hp] + c_ref[hp, c]
        return carry

    lax.fori_loop(0, n_chunks, phase_b, 0)

    gmean = gmean_ref[...]

    def finish(hp, carry):
        y = y_ref[hp]
        d = y - _dot_split(y, gmean)
        var = _dot_split(d * d, gmean)
        y = d * lax.rsqrt(var + GN_EPS) * lng_ref[hp] + lnb_ref[hp]
        out_ref[0, hp] = (y + bonus_ref[0, hp]) * g_ref[0, hp]
        return carry

    lax.fori_loop(0, HEAD_PAIRS, finish, 0)


def _rwkv_scan(r, k, v, kn, a, cs, bonus, g, lnx_g, lnx_b, tc):
    B, HP, T, _ = r.shape
    gmean = _group_matrix(LANES, HEAD_DIM, 1.0 / HEAD_DIM)
    n_chunks = tc // CHUNK
    return pl.pallas_call(
        _rwkv_scan_kernel,
        grid=(B, T // tc),
        in_specs=[_slab_spec(tc)] * 8 + [_const_spec((HP, 1, LANES))] * 2 + [_const_spec((LANES, LANES))],
        out_specs=_slab_spec(tc),
        out_shape=jax.ShapeDtypeStruct((B, HP, T, LANES), F32),
        scratch_shapes=[pltpu.VMEM((HP, LANES, LANES), F32),
                        pltpu.VMEM((HP, tc, LANES), BF16),
                        pltpu.VMEM((HP, tc, LANES), F32),
                        pltpu.VMEM((HP, n_chunks, LANES, LANES), BF16),
                        pltpu.VMEM((HP, n_chunks, LANES, LANES), F32)],
        compiler_params=_params("arbitrary", "arbitrary"),
        name="rwkv_scan",
    )(r, k, v, kn, a, cs, bonus, g, lnx_g.reshape(HP, 1, LANES), lnx_b.reshape(HP, 1, LANES), gmean)


def _diff_in_kernel(x_ref, ng_ref, w_ref, qg_ref, kg_ref, gmean_ref, q_out, k_out, v_out, qm_out):
    C = MIX_WIDTH
    h = _rms_rows(x_ref[0], ng_ref[...]).astype(BF16)
    proj = _dot(h, w_ref[...])
    gmean = gmean_ref[...]

    def head_norm(t):
        ms = jnp.concatenate(
            [_dot_split(jnp.square(t[:, i:i + GROUP_COLS]), gmean) for i in range(0, C, GROUP_COLS)], axis=1)
        return t * lax.rsqrt(ms + RMS_EPS)

    q_out[0] = (head_norm(proj[:, :C]) * qg_ref[...]).astype(BF16)
    k_out[0] = (head_norm(proj[:, C:2 * C]) * kg_ref[...]).astype(BF16)
    v_out[0] = proj[:, 2 * C:3 * C].astype(BF16)
    qm_out[0] = proj[:, 3 * C:]


def _diff_in(x, norm_g, w_in, q_g, k_g, tm):
    B, T, D = x.shape
    C = MIX_WIDTH
    n_in = w_in.shape[1]
    gmean = _group_matrix(GROUP_COLS, HEAD_DIM, 1.0 / HEAD_DIM)
    qg = jnp.tile(q_g.reshape(-1), HEAD_PAIRS).reshape(1, C) * (HEAD_DIM ** -0.5 * LOG2_E)
    kg = jnp.tile(k_g.reshape(-1), HEAD_PAIRS).reshape(1, C)
    tile = pl.BlockSpec((1, tm, C), lambda b, t: (b, t, 0))
    big = jax.ShapeDtypeStruct((B, T, C), BF16)
    return pl.pallas_call(
        _diff_in_kernel,
        grid=(B, T // tm),
        in_specs=[pl.BlockSpec((1, tm, D), lambda b, t: (b, t, 0)),
                  _const_spec((1, D)), _const_spec((D, n_in)),
                  _const_spec((1, C)), _const_spec((1, C)), _const_spec((GROUP_COLS, GROUP_COLS))],
        out_specs=[tile] * 3 + [pl.BlockSpec((1, tm, MEM_WIDTH), lambda b, t: (b, t, 0))],
        out_shape=[big] * 3 + [jax.ShapeDtypeStruct((B, T, MEM_WIDTH), F32)],
        compiler_params=_params("arbitrary", "arbitrary"),
        name="diff_in",
    )(x, norm_g.reshape(1, D), w_in.astype(BF16), qg, kg, gmean)


def _diff_flash_kernel(q_ref, k_ref, v_ref, qfeat_ref, kfeat_ref, ones_ref,
                       lq1_ref, lk1_ref, lq2_ref, lk2_ref, sg_ref,
                       out_ref, sa_ref, sb_ref, *, lambda_init, tq, tk):
    seq = q_ref.shape[1]
    lam = (jnp.exp(jnp.sum(lq1_ref[...] * lk1_ref[...], axis=-1, keepdims=True))
           - jnp.exp(jnp.sum(lq2_ref[...] * lk2_ref[...], axis=-1, keepdims=True)) + lambda_init)
    lane = lax.broadcasted_iota(jnp.int32, (tq, LANES), 1)
    th = tk // 2
    col = lax.broadcasted_iota(jnp.int32, (1, th), 1)
    q_loc = lax.broadcasted_iota(jnp.int32, (2 * tq, 1), 0) & (tq - 1)

    def q_tile(qi, tile_carry):
        q_start = pl.multiple_of(qi * tq, tq)
        q = q_ref[0, pl.ds(q_start, tq), :]
        zero = jnp.zeros_like(q)
        qs = jnp.concatenate(
            [jnp.where(lane < HEAD_DIM, q, zero), jnp.where(lane < HEAD_DIM, zero, q)], axis=0)
        qs = jnp.concatenate([qs, jnp.broadcast_to(qfeat_ref[0], (2 * tq, LANES))], axis=1)

        def scores(k_start):
            rows = pl.ds(pl.multiple_of(k_start, th), th)
            return _dot_nt(qs, jnp.concatenate([k_ref[0, rows, :], kfeat_ref[rows, :]], axis=1))

        def softmax_pv(s, k_start, carry, masked):
            m, acc = carry
            if masked:
                s = jnp.where(col + (k_start - q_start) <= q_loc, s, NEG_BIG)
            m_new = jnp.maximum(m, jnp.max(s, axis=-1, keepdims=True))
            p = jnp.exp2(s - m_new).astype(BF16)
            vb = v_ref[0, pl.ds(pl.multiple_of(k_start, th), th), :]
            pv = _dot(p, jnp.concatenate([vb, ones_ref[...]], axis=1))
            return m_new, jnp.exp2(m - m_new) * acc + pv

        def full_block(j, carry):
            k0 = j * tk
            sb_ref[...] = scores(k0 + th)
            carry = softmax_pv(sa_ref[...], k0, carry, False)
            sa_ref[...] = scores(k0 + tk)
            return softmax_pv(sb_ref[...], k0 + th, carry, False)

        n_full = q_start // tk
        sa_ref[...] = scores(0)
        carry = (jnp.full((2 * tq, 1), NEG_BIG, F32), jnp.zeros((2 * tq, 2 * LANES), F32))
        carry = lax.fori_loop(0, n_full, full_block, carry)
        k0 = n_full * tk
        sb_ref[...] = scores(k0 + th)
        carry = softmax_pv(sa_ref[...], k0, carry, True)
        _, acc = lax.cond(q_start + tq > k0 + th,
                          lambda c: softmax_pv(sb_ref[...], k0 + th, c, True),
                          lambda c: c, carry)

        o = acc[:, :LANES] / acc[:, LANES:LANES + 1]
        o = o[:tq] - lam * o[tq:]
        out_ref[0, 0, pl.ds(q_start, tq), :] = _rms_rows(o, sg_ref[...]) * (1.0 - lambda_init)
        return tile_carry

    lax.fori_loop(0, seq // tq, q_tile, 0)


def _diff_flash(q, k, v, lq1, lk1, lq2, lk2, subln_g, lambda_init, tq, tk):
    B, T, C = q.shape
    H = C // LANES
    slopes = jnp.exp2(-8.0 * jnp.arange(1, H + 1, dtype=F32) / H) * LOG2_E
    assert T // FEAT_RADIX <= 256
    s1 = slopes.astype(BF16)
    r1 = slopes - s1.astype(F32)
    s2 = r1.astype(BF16)
    s3 = (r1 - s2.astype(F32)).astype(BF16)
    pieces = jnp.stack([s1, s2, s3], axis=1).astype(F32)
    qfeat = jnp.concatenate([pieces * FEAT_RADIX, pieces, jnp.zeros((H, LANES - 6), F32)], axis=1)
    qfeat = qfeat.astype(BF16).reshape(H, 1, LANES)
    pos = jnp.arange(T)
    hi = jnp.broadcast_to((pos // FEAT_RADIX)[:, None], (T, 3))
    lo = jnp.broadcast_to((pos % FEAT_RADIX)[:, None], (T, 3))
    kfeat = jnp.concatenate([hi, lo, jnp.zeros((T, LANES - 6), jnp.int32)], axis=1).astype(BF16)
    ones = jnp.zeros((tk // 2, LANES), BF16).at[:, 0].set(1.0)
    lvec = _const_spec((1, HEAD_DIM))
    seq = pl.BlockSpec((1, T, LANES), lambda b, h: (b, 0, h))
    return pl.pallas_call(
        functools.partial(_diff_flash_kernel, lambda_init=lambda_init, tq=tq, tk=tk),
        grid=(B, H),
        in_specs=[seq, seq, seq,
                  pl.BlockSpec((1, 1, LANES), lambda b, h: (h, 0, 0)), _const_spec((T, LANES)),
                  _const_spec((tk // 2, LANES)),
                  lvec, lvec, lvec, lvec, _const_spec((1, LANES))],
        out_specs=pl.BlockSpec((1, 1, T, LANES), lambda b, h: (b, h, 0, 0)),
        out_shape=jax.ShapeDtypeStruct((B, H, T, LANES), F32),
        scratch_shapes=[pltpu.VMEM((2 * tq, tk // 2), F32)] * 2,
        compiler_params=_params("arbitrary", "arbitrary"),
        name="diff_flash",
    )(q, k, v, qfeat, kfeat, ones,
      lq1.reshape(1, -1), lk1.reshape(1, -1), lq2.reshape(1, -1), lk2.reshape(1, -1),
      subln_g.reshape(1, LANES))


def _attn_out_kernel(mix_ref, qm_ref, km_ref, vm_ref, x_ref, qg_ref, kg_ref, wa_ref, wb_ref, gmean_ref,
                     out_ref):
    qm = qm_ref[0]
    ms = _dot_split(qm * qm, gmean_ref[...])
    qn = qm * lax.rsqrt(ms + RMS_EPS) * qg_ref[...]
    kb = (km_ref[0] * kg_ref[...]).astype(BF16)
    vb = vm_ref[0].astype(BF16)
    lane = lax.broadcasted_iota(jnp.int32, qn.shape, 1)
    mem = jnp.zeros_like(qn)
    for hd in range(MEM_HEADS):
        in_head = (lane >= hd * HEAD_DIM) & (lane < (hd + 1) * HEAD_DIM)
        s = _dot_nt(jnp.where(in_head, qn, 0.0).astype(BF16), kb)
        p = jnp.exp(s - jnp.max(s, axis=-1, keepdims=True))
        o = _dot(p.astype(BF16), vb) / jnp.sum(p, axis=-1, keepdims=True)
        mem = jnp.where(in_head, o, mem)
    mix = jnp.concatenate([mix_ref[0, hp] for hp in range(HEAD_PAIRS)], axis=1)
    y = _dot(mix.astype(BF16), wa_ref[...]) + _dot(mem.astype(BF16), wb_ref[...])
    out_ref[0] = x_ref[0] + y


def _attn_out(mix, qm, k_mem, v_mem, x, q_g, k_g, w_out, tm):
    B, T, D = x.shape
    M = k_mem.shape[1]
    gmean = _group_matrix(MEM_WIDTH, HEAD_DIM, 1.0 / HEAD_DIM)
    qg = jnp.tile(q_g, MEM_HEADS).reshape(1, MEM_WIDTH) * HEAD_DIM ** -0.5
    kg = jnp.tile(k_g, MEM_HEADS).reshape(1, MEM_WIDTH)
    w = w_out.astype(BF16)
    rows = lambda n: pl.BlockSpec((1, tm, n), lambda b, t: (b, t, 0))
    memspec = pl.BlockSpec((1, M, MEM_WIDTH), lambda b, t: (b, 0, 0))
    return pl.pallas_call(
        _attn_out_kernel,
        grid=(B, T // tm),
        in_specs=[_slab_spec(tm), rows(MEM_WIDTH), memspec, memspec, rows(D),
                  _const_spec((1, MEM_WIDTH)), _const_spec((1, MEM_WIDTH)),
                  _const_spec((MIX_WIDTH, D)), _const_spec((MEM_WIDTH, D)),
                  _const_spec((MEM_WIDTH, MEM_WIDTH))],
        out_specs=rows(D),
        out_shape=jax.ShapeDtypeStruct((B, T, D), F32),
        compiler_params=_params("arbitrary", "arbitrary"),
        name="attn_out",
    )(mix, qm, k_mem, v_mem, x, qg, kg, w[:MIX_WIDTH], w[MIX_WIDTH:], gmean)


def _ffn_kernel(x_ref, ng_ref, w1_ref, w2_ref, out_ref, *, ff_chunk):
    x = x_ref[0]
    h = _rms_rows(x, ng_ref[...]).astype(BF16)
    acc = x
    for c in range(0, w1_ref.shape[1], ff_chunk):
        u = jnp.maximum(_dot(h, w1_ref[:, c:c + ff_chunk]), 0.0)
        acc = acc + _dot((u * u).astype(BF16), w2_ref[c:c + ff_chunk, :])
    out_ref[0] = acc


def _ffn(x, norm_g, w1, w2, tm):
    B, T, D = x.shape
    F = w1.shape[1]
    rows = pl.BlockSpec((1, tm, D), lambda b, t: (b, t, 0))
    return pl.pallas_call(
        functools.partial(_ffn_kernel, ff_chunk=D),
        grid=(B, T // tm),
        in_specs=[rows, _const_spec((1, D)), _const_spec((D, F)), _const_spec((F, D))],
        out_specs=rows,
        out_shape=jax.ShapeDtypeStruct((B, T, D), F32),
        compiler_params=_params("arbitrary", "arbitrary"),
        name="ffn",
    )(x, norm_g.reshape(1, D), w1.astype(BF16), w2.astype(BF16))


def _diff_lambda_init(layer):
    return 0.8 - 0.6 * math.exp(-0.3 * layer)


def kernel(x, mem, norm_mix_g, norm_ffn_g, w_out, w_ff1, w_ff2, mem_norm_g, w_mem_kv, mem_q_norm_g, mem_k_norm_g, rw_in, rw_mu, rw_w0, rw_w2, rw_a0, rw_a2, rw_g2, rw_k_k, rw_k_a, rw_r_k, rw_lnx_g, rw_lnx_b, df_in, df_q_norm_g, df_k_norm_g, df_lq1, df_lk1, df_lq2, df_lk2, df_subln_g):
    T = x.shape[1]
    tm = min(T, 256)
    depth = norm_mix_g.shape[0]
    k_mem, v_mem = _mem_kv(mem, mem_norm_g, w_mem_kv)
    for layer in range(depth):
        j = layer // 2
        if layer % 2 == 0:
            r, k, v, kn, a, cs, bonus, g, qm = _rwkv_in(
                x, norm_mix_g[layer], rw_in[j], rw_mu[j], rw_w0[j], rw_w2[j], rw_a0[j], rw_a2[j],
                rw_g2[j], rw_k_k[j], rw_k_a[j], rw_r_k[j], tm)
            mix = _rwkv_scan(r, k, v, kn, a, cs, bonus, g, rw_lnx_g[j], rw_lnx_b[j], min(T, 512))
        else:
            q, k, v, qm = _diff_in(x, norm_mix_g[layer], df_in[j], df_q_norm_g[j], df_k_norm_g[j], tm)
            mix = _diff_flash(q, k, v, df_lq1[j], df_lk1[j], df_lq2[j], df_lk2[j], df_subln_g[j],
                              _diff_lambda_init(layer), min(T, 512), min(T, 1024))
        x = _attn_out(mix, qm, k_mem, v_mem, x, mem_q_norm_g[layer], mem_k_norm_g[layer], w_out[layer], tm)
        x = _ffn(x, norm_ffn_g[layer], w_ff1[layer], w_ff2[layer], min(T, 512))
    return x
```

```python
import functools
import math

import jax
import jax.numpy as jnp
from jax import lax
from jax.experimental import pallas as pl
from jax.experimental.pallas import tpu as pltpu

F32 = jnp.float32
BF16 = jnp.bfloat16

D_MODEL = 1024
HEAD_DIM = 64
MIX_WIDTH = 768
MEM_WIDTH = 256
MEM_HEADS = 4
DECAY_LORA = 64
ICLR_LORA = 64
GATE_LORA = 128
RWKV_COLS = 3 * MIX_WIDTH + DECAY_LORA + ICLR_LORA + GATE_LORA
D_FF = 4 * D_MODEL
RMS_EPS = 1e-6
GN_EPS = 64e-5

LANES = 128
HEAD_PAIRS = MIX_WIDTH // LANES
GROUP_COLS = 256
CHUNK = 64
SCAN_PAIRS = 3
NEG_BIG = -1e30
LOG2_E = math.log2(math.e)
FEAT_RADIX = 32
VMEM_LIMIT = 56 * 1024 * 1024


def _dot(a, b):
    return jnp.dot(a, b, preferred_element_type=F32)


def _dot_nt(a, b):
    return lax.dot_general(a, b, (((1,), (1,)), ((), ())), preferred_element_type=F32)


def _dot_split(x, m):
    hi = x.astype(BF16)
    lo = (x - hi.astype(F32)).astype(BF16)
    return _dot(hi, m) + _dot(lo, m)


def _dot_split3(m, x):
    hi = x.astype(BF16)
    r1 = x - hi.astype(F32)
    mid = r1.astype(BF16)
    lo = (r1 - mid.astype(F32)).astype(BF16)
    return _dot(m, hi) + _dot(m, mid) + _dot(m, lo)


def _sigmoid(x):
    return 1.0 / (1.0 + jnp.exp(-x))


def _rms_rows(x, g):
    return x * lax.rsqrt(jnp.mean(x * x, axis=-1, keepdims=True) + RMS_EPS) * g


def _group_matrix(n, group, value):
    i = jnp.arange(n) // group
    return jnp.where(i[:, None] == i[None, :], value, 0.0).astype(BF16)


def _params(*sem):
    return pltpu.CompilerParams(dimension_semantics=sem, vmem_limit_bytes=VMEM_LIMIT)


def _const_spec(shape):
    nd = len(shape)
    return pl.BlockSpec(shape, lambda *_: (0,) * nd)


def _slab_spec(tm):
    return pl.BlockSpec((1, HEAD_PAIRS, tm, LANES), lambda b, t: (b, 0, t, 0))


def _mem_kv_kernel(mem_ref, g_ref, w_ref, gm_ref, k_out, v_out):
    h = _rms_rows(mem_ref[0], g_ref[...])
    kv = _dot(h.astype(BF16), w_ref[...])
    k = kv[:, :MEM_WIDTH]
    ms = _dot_split(k * k, gm_ref[...])
    k_out[0] = k * lax.rsqrt(ms + RMS_EPS)
    v_out[0] = kv[:, MEM_WIDTH:]


def _mem_kv(mem, mem_norm_g, w_mem_kv):
    B, M, D = mem.shape
    gm = _group_matrix(MEM_WIDTH, HEAD_DIM, 1.0 / HEAD_DIM)
    out = jax.ShapeDtypeStruct((B, M, MEM_WIDTH), F32)
    return pl.pallas_call(
        _mem_kv_kernel,
        grid=(B,),
        in_specs=[pl.BlockSpec((1, M, D), lambda b: (b, 0, 0)),
                  _const_spec((1, D)), _const_spec((D, 2 * MEM_WIDTH)),
                  _const_spec((MEM_WIDTH, MEM_WIDTH))],
        out_specs=[pl.BlockSpec((1, M, MEM_WIDTH), lambda b: (b, 0, 0))] * 2,
        out_shape=[out, out],
        compiler_params=_params("arbitrary"),
        name="mem_kv",
    )(mem, mem_norm_g.reshape(1, D), w_mem_kv.astype(BF16), gm)


def _rwkv_in_kernel(x_ref, ng_ref, w_ref, mu_ref, w0_ref, w2_ref, a0_ref, a2_ref, g2_ref,
                    kk_ref, ka_ref, rk_ref, gsum_ref, tri_ref,
                    r_out, k_out, v_out, kn_out, a_out, cs_out, bonus_out, g_out, qm_out,
                    carry_ref):
    C = MIX_WIDTH
    tm = x_ref.shape[1]

    @pl.when(pl.program_id(1) == 0)
    def _():
        carry_ref[...] = jnp.zeros_like(carry_ref)

    h = _rms_rows(x_ref[0], ng_ref[...]).astype(BF16)
    proj = _dot(h, w_ref[...])
    cur = proj[:, :RWKV_COLS]
    row = lax.broadcasted_iota(jnp.int32, (tm, 1), 0)
    prev = jnp.where(row == 0, carry_ref[0:1, :], pltpu.roll(cur, 1, axis=0))
    carry_ref[0:1, :] = cur[tm - 1:tm, :]
    slab = cur + (prev - cur) * mu_ref[...]

    r = slab[:, 0:C]
    k = slab[:, C:2 * C]
    v = slab[:, 2 * C:3 * C]
    wa = slab[:, 3 * C:3 * C + DECAY_LORA + ICLR_LORA]
    gd = slab[:, 3 * C + DECAY_LORA + ICLR_LORA:RWKV_COLS]

    d = w0_ref[...] + _dot(jnp.tanh(wa).astype(BF16), w2_ref[...])
    log_decay = -math.exp(-0.5) * _sigmoid(d)
    a = _sigmoid(a0_ref[...] + _dot(wa.astype(BF16), a2_ref[...]))
    g = _dot(_sigmoid(gd).astype(BF16), g2_ref[...])

    gsum = gsum_ref[...]

    def head_sum(t):
        return jnp.concatenate(
            [_dot(t[:, i:i + GROUP_COLS].astype(BF16), gsum) for i in range(0, C, GROUP_COLS)], axis=1)

    kn = k * kk_ref[...]
    kn = kn * lax.rsqrt(jnp.maximum(head_sum(kn * kn), 1e-24))
    k2 = k * (1.0 + (a - 1.0) * ka_ref[...])
    bonus = head_sum(r * k2 * rk_ref[...]) * v

    cs = jnp.concatenate([_dot_split3(tri_ref[...], log_decay[i:i + GROUP_COLS])
                          for i in range(0, tm, GROUP_COLS)], axis=0)
    outs = ((r_out, r), (k_out, k2), (v_out, v), (kn_out, kn), (a_out, a), (cs_out, cs),
            (bonus_out, bonus), (g_out, g))
    for ref, val in outs:
        for hp in range(HEAD_PAIRS):
            ref[0, hp] = val[:, hp * LANES:(hp + 1) * LANES]
    qm_out[0] = proj[:, RWKV_COLS:]


def _rwkv_in(x, norm_g, w_in, mu, w0, w2, a0, a2, g2, k_k, k_a, r_k, tm):
    B, T, D = x.shape
    C = MIX_WIDTH
    n_in = w_in.shape[1]
    zeros = jnp.zeros((DECAY_LORA, C), F32)
    w2p = jnp.concatenate([w2, zeros], axis=0).astype(BF16)
    a2p = jnp.concatenate([zeros, a2], axis=0).astype(BF16)
    gsum = _group_matrix(GROUP_COLS, HEAD_DIM, 1.0)
    t_idx = jnp.arange(GROUP_COLS)
    tri = ((t_idx[:, None] // CHUNK == t_idx[None, :] // CHUNK)
           & (t_idx[None, :] <= t_idx[:, None])).astype(BF16)
    row = lambda p: p.reshape(1, -1)
    big = jax.ShapeDtypeStruct((B, HEAD_PAIRS, T, LANES), F32)
    return pl.pallas_call(
        _rwkv_in_kernel,
        grid=(B, T // tm),
        in_specs=[pl.BlockSpec((1, tm, D), lambda b, t: (b, t, 0)),
                  _const_spec((1, D)), _const_spec((D, n_in)), _const_spec((1, RWKV_COLS)),
                  _const_spec((1, C)), _const_spec((2 * DECAY_LORA, C)),
                  _const_spec((1, C)), _const_spec((2 * ICLR_LORA, C)),
                  _const_spec((GATE_LORA, C)),
                  _const_spec((1, C)), _const_spec((1, C)), _const_spec((1, C)),
                  _const_spec((GROUP_COLS, GROUP_COLS)), _const_spec((GROUP_COLS, GROUP_COLS))],
        out_specs=[_slab_spec(tm)] * 8 + [pl.BlockSpec((1, tm, MEM_WIDTH), lambda b, t: (b, t, 0))],
        out_shape=[big] * 8 + [jax.ShapeDtypeStruct((B, T, MEM_WIDTH), F32)],
        scratch_shapes=[pltpu.VMEM((8, RWKV_COLS), F32)],
        compiler_params=_params("arbitrary", "arbitrary"),
        name="rwkv_in",
    )(x, row(norm_g), w_in.astype(BF16), row(mu), row(w0), w2p, row(a0), a2p, g2.astype(BF16),
      row(k_k), row(k_a), row(r_k), gsum, tri)


def _rwkv_scan_kernel(r_ref, k_ref, v_ref, kn_ref, a_ref, cs_ref, bonus_ref, g_ref,
                      lng_ref, lnb_ref, gmean_ref, out_ref,
                      s_ref, rhat_ref, y_ref, m_ref, c_ref):
    L = CHUNK
    tc = r_ref.shape[2]
    n_chunks = tc // L

    @pl.when(pl.program_id(1) == 0)
    def _():
        s_ref[...] = jnp.zeros_like(s_ref)

    lane = lax.broadcasted_iota(jnp.int32, (L, LANES), 1)
    row = lax.broadcasted_iota(jnp.int32, (L, LANES), 0)
    head0 = lane < HEAD_DIM
    src = lane & (HEAD_DIM - 1)
    strict = row > src
    incl = row >= src
    r2 = lax.broadcasted_iota(jnp.int32, (LANES, LANES), 0)
    c2 = lax.broadcasted_iota(jnp.int32, (LANES, LANES), 1)
    same_head = (r2 >> 6) == (c2 >> 6)
    eye = r2 == c2
    eye_pk = jnp.where(row == src, 1.0, 0.0)
    first_row = (lax.broadcasted_iota(jnp.int32, (tc, LANES), 0) & (L - 1)) == 0

    def stack2(t):
        return jnp.concatenate([jnp.where(head0, t, 0.0), jnp.where(head0, 0.0, t)], axis=0)

    def side2(x, y):
        return jnp.concatenate([stack2(x), stack2(y)], axis=1).astype(BF16)

    def phase_a(trip, carry):
        hps = [trip * SCAN_PAIRS + i for i in range(SCAN_PAIRS)]
        cut = lambda t: [t[c * L:(c + 1) * L] for c in range(n_chunks)]
        rt, kt, at, bt, vv, w_last = [], [], [], [], [], []
        for hp in hps:
            kn = kn_ref[0, hp]
            cs = cs_ref[0, hp]
            cs_prev = jnp.where(first_row, 0.0, pltpu.roll(cs, 1, axis=0))
            w_t = jnp.exp(cs)
            w_inv = jnp.exp(-cs)
            rt += cut(r_ref[0, hp] * w_t)
            kt += cut(k_ref[0, hp] * w_inv)
            at += cut(-kn * jnp.exp(cs_prev))
            bt += cut(kn * a_ref[0, hp] * w_inv)
            vv += cut(v_ref[0, hp])
            w_last += [w_t[(c + 1) * L - 1:(c + 1) * L] for c in range(n_chunks)]
        chunks = range(SCAN_PAIRS * n_chunks)

        sc = [_dot_nt(jnp.concatenate([at[c], rt[c]], axis=0).astype(BF16),
                      jnp.concatenate([stack2(bt[c]), stack2(kt[c])], axis=0).astype(BF16))
              for c in chunks]
        a_ab = [jnp.where(strict, s[:L, :LANES], 0.0) for s in sc]
        a_ak = [jnp.where(strict, s[:L, LANES:], 0.0).astype(BF16) for s in sc]
        a_rb = [jnp.where(incl, s[L:, :LANES], 0.0).astype(BF16) for s in sc]
        a_rk = [jnp.where(incl, s[L:, LANES:], 0.0).astype(BF16) for s in sc]

        npow = a_ab
        inv = [eye_pk + n for n in npow]
        nbd = [stack2(n).astype(BF16) for n in npow]
        npow = [_dot(n.astype(BF16), b) for n, b in zip(npow, nbd)]
        for _ in range(int(math.log2(L)) - 2):
            nbd = [stack2(n).astype(BF16) for n in npow]
            both = [_dot(jnp.concatenate([n, i], axis=0).astype(BF16), b)
                    for n, i, b in zip(npow, inv, nbd)]
            npow = [t[:L] for t in both]
            inv = [i + t[L:] for i, t in zip(inv, both)]
        nbd = [stack2(n).astype(BF16) for n in npow]
        t_pk = [(i + _dot(i.astype(BF16), b)).astype(BF16) for i, b in zip(inv, nbd)]

        v2 = [stack2(t).astype(BF16) for t in vv]
        av = [_dot(jnp.concatenate([a_ak[c], a_rk[c]], axis=0), v2[c]) for c in chunks]
        q0 = [t[:L] for t in av]
        arkv = [t[L:] for t in av]
        pq = [_dot(t_pk[c], side2(at[c], q0[c])) for c in chunks]
        ar = [_dot(a_rb[c], side2(pq[c][:, :LANES], pq[c][:, LANES:])) for c in chunks]
        mc = [_dot(jnp.concatenate([bt[c] * w_last[c], kt[c] * w_last[c]], axis=0).T.astype(BF16),
                   jnp.concatenate(
                       [pq[c], jnp.concatenate([jnp.zeros_like(vv[c]), vv[c]], axis=1)], axis=0).astype(BF16))
              for c in chunks]
        for i in chunks:
            hp, c = hps[i // n_chunks], i % n_chunks
            rows = pl.ds(c * L, L)
            rhat_ref[hp, rows, :] = (rt[i] + ar[i][:, :LANES]).astype(BF16)
            y_ref[hp, rows, :] = ar[i][:, LANES:] + arkv[i]
            m_ref[hp, c] = (jnp.where(same_head, mc[i][:, :LANES], 0.0)
                            + jnp.where(eye, w_last[i], 0.0)).astype(BF16)
            c_ref[hp, c] = jnp.where(same_head, mc[i][:, LANES:], 0.0)
        return carry

    lax.fori_loop(0, HEAD_PAIRS // SCAN_PAIRS, phase_a, 0)

    def phase_b(c, carry):
        rows = pl.ds(pl.multiple_of(c * L, L), L)
        pairs = range(HEAD_PAIRS)
        sb = [s_ref[hp].astype(BF16) for hp in pairs]
        y = [_dot(rhat_ref[hp, rows, :], sb[hp]) for hp in pairs]
        s_new = [_dot(m_ref[hp, c], sb[hp]) for hp in pairs]
        for hp in pairs:
            y_ref[hp, rows, :] = y_ref[hp, rows, :] + y[hp]
            s_ref[hp] = s_new[hp] + c_ref[hp, c]
        return carry

    lax.fori_loop(0, n_chunks, phase_b, 0)

    gmean2 = gmean_ref[...]

    def head_mean(t):
        hi = t.astype(BF16)
        lo = (t - hi.astype(F32)).astype(BF16)
        return _dot(jnp.concatenate([hi, lo], axis=1), gmean2)

    pairs = range(HEAD_PAIRS)
    ys = [y_ref[hp] for hp in pairs]
    ds = [y - head_mean(y) for y in ys]
    var = [head_mean(d * d) for d in ds]
    for hp in pairs:
        y = ds[hp] * lax.rsqrt(var[hp] + GN_EPS) * lng_ref[hp] + lnb_ref[hp]
        out_ref[0, hp] = (y + bonus_ref[0, hp]) * g_ref[0, hp]


def _rwkv_scan(r, k, v, kn, a, cs, bonus, g, lnx_g, lnx_b, tc):
    B, HP, T, _ = r.shape
    gmean = _group_matrix(LANES, HEAD_DIM, 1.0 / HEAD_DIM)
    gmean = jnp.concatenate([gmean, gmean], axis=0)
    n_chunks = tc // CHUNK
    return pl.pallas_call(
        _rwkv_scan_kernel,
        grid=(B, T // tc),
        in_specs=[_slab_spec(tc)] * 8 + [_const_spec((HP, 1, LANES))] * 2 + [_const_spec((2 * LANES, LANES))],
        out_specs=_slab_spec(tc),
        out_shape=jax.ShapeDtypeStruct((B, HP, T, LANES), F32),
        scratch_shapes=[pltpu.VMEM((HP, LANES, LANES), F32),
                        pltpu.VMEM((HP, tc, LANES), BF16),
                        pltpu.VMEM((HP, tc, LANES), F32),
                        pltpu.VMEM((HP, n_chunks, LANES, LANES), BF16),
                        pltpu.VMEM((HP, n_chunks, LANES, LANES), F32)],
        compiler_params=_params("arbitrary", "arbitrary"),
        name="rwkv_scan",
    )(r, k, v, kn, a, cs, bonus, g, lnx_g.reshape(HP, 1, LANES), lnx_b.reshape(HP, 1, LANES), gmean)


def _diff_in_kernel(x_ref, ng_ref, w_ref, qg_ref, kg_ref, gmean_ref, q_out, k_out, v_out, qm_out):
    C = MIX_WIDTH
    h = _rms_rows(x_ref[0], ng_ref[...]).astype(BF16)
    proj = _dot(h, w_ref[...])
    gmean = gmean_ref[...]

    def head_norm(t):
        ms = jnp.concatenate(
            [_dot(jnp.square(t[:, i:i + GROUP_COLS]).astype(BF16), gmean)
             for i in range(0, C, GROUP_COLS)], axis=1)
        return t * lax.rsqrt(ms + RMS_EPS)

    q_out[0] = (head_norm(proj[:, :C]) * qg_ref[...]).astype(BF16)
    k_out[0] = (head_norm(proj[:, C:2 * C]) * kg_ref[...]).astype(BF16)
    v_out[0] = proj[:, 2 * C:3 * C].astype(BF16)
    qm_out[0] = proj[:, 3 * C:]


def _diff_in(x, norm_g, w_in, q_g, k_g, tm):
    B, T, D = x.shape
    C = MIX_WIDTH
    n_in = w_in.shape[1]
    gmean = _group_matrix(GROUP_COLS, HEAD_DIM, 1.0 / HEAD_DIM)
    qg = jnp.tile(q_g.reshape(-1), HEAD_PAIRS).reshape(1, C) * (HEAD_DIM ** -0.5 * LOG2_E)
    kg = jnp.tile(k_g.reshape(-1), HEAD_PAIRS).reshape(1, C)
    tile = pl.BlockSpec((1, tm, C), lambda b, t: (b, t, 0))
    big = jax.ShapeDtypeStruct((B, T, C), BF16)
    return pl.pallas_call(
        _diff_in_kernel,
        grid=(B, T // tm),
        in_specs=[pl.BlockSpec((1, tm, D), lambda b, t: (b, t, 0)),
                  _const_spec((1, D)), _const_spec((D, n_in)),
                  _const_spec((1, C)), _const_spec((1, C)), _const_spec((GROUP_COLS, GROUP_COLS))],
        out_specs=[tile] * 3 + [pl.BlockSpec((1, tm, MEM_WIDTH), lambda b, t: (b, t, 0))],
        out_shape=[big] * 3 + [jax.ShapeDtypeStruct((B, T, MEM_WIDTH), F32)],
        compiler_params=_params("arbitrary", "arbitrary"),
        name="diff_in",
    )(x, norm_g.reshape(1, D), w_in.astype(BF16), qg, kg, gmean)


def _diff_flash_kernel(q_ref, k_ref, v_ref, qfeat_ref, kfeat_ref, ones_ref,
                       lq1_ref, lk1_ref, lq2_ref, lk2_ref, sg_ref,
                       out_ref, sa_ref, sb_ref, *, lambda_init, tq, tk):
    seq = q_ref.shape[1]
    lam = (jnp.exp(jnp.sum(lq1_ref[...] * lk1_ref[...], axis=-1, keepdims=True))
           - jnp.exp(jnp.sum(lq2_ref[...] * lk2_ref[...], axis=-1, keepdims=True)) + lambda_init)
    lane = lax.broadcasted_iota(jnp.int32, (tq, LANES), 1)
    th = tk // 2
    col = lax.broadcasted_iota(jnp.int32, (1, th), 1)
    q_loc = lax.broadcasted_iota(jnp.int32, (2 * tq, 1), 0) & (tq - 1)

    def stacked_queries(q_start):
        q = q_ref[0, pl.ds(pl.multiple_of(q_start, tq), tq), :]
        zero = jnp.zeros_like(q)
        qs = jnp.concatenate(
            [jnp.where(lane < HEAD_DIM, q, zero), jnp.where(lane < HEAD_DIM, zero, q)], axis=0)
        return jnp.concatenate([qs, jnp.broadcast_to(qfeat_ref[0], (2 * tq, LANES))], axis=1)

    def tile_scores(qs, k_start):
        rows = pl.ds(pl.multiple_of(k_start, th), th)
        return _dot_nt(qs, jnp.concatenate([k_ref[0, rows, :], kfeat_ref[rows, :]], axis=1))

    sa_ref[...] = tile_scores(stacked_queries(0), 0)

    def q_tile(qi, tile_carry):
        q_start = pl.multiple_of(qi * tq, tq)
        qs = stacked_queries(q_start)
        scores = functools.partial(tile_scores, qs)

        def softmax_pv(s, k_start, carry, masked):
            m, acc = carry
            if masked:
                s = jnp.where(col + (k_start - q_start) <= q_loc, s, NEG_BIG)
            m_new = jnp.maximum(m, jnp.max(s, axis=-1, keepdims=True))
            p = jnp.exp2(s - m_new).astype(BF16)
            vb = v_ref[0, pl.ds(pl.multiple_of(k_start, th), th), :]
            pv = _dot(p, jnp.concatenate([vb, ones_ref[...]], axis=1))
            return m_new, jnp.exp2(m - m_new) * acc + pv

        def full_block(j, carry):
            k0 = j * tk
            sb_ref[...] = scores(k0 + th)
            carry = softmax_pv(sa_ref[...], k0, carry, False)
            sa_ref[...] = scores(k0 + tk)
            return softmax_pv(sb_ref[...], k0 + th, carry, False)

        n_full = q_start // tk
        carry = (jnp.full((2 * tq, 1), NEG_BIG, F32), jnp.zeros((2 * tq, 2 * LANES), F32))
        carry = lax.fori_loop(0, n_full, full_block, carry)
        k0 = n_full * tk
        second_half = q_start + tq > k0 + th

        @pl.when(second_half)
        def _():
            sb_ref[...] = scores(k0 + th)

        carry = softmax_pv(sa_ref[...], k0, carry, True)
        sa_ref[...] = tile_scores(stacked_queries(jnp.minimum(q_start + tq, seq - tq)), 0)
        _, acc = lax.cond(second_half,
                          lambda c: softmax_pv(sb_ref[...], k0 + th, c, True),
                          lambda c: c, carry)

        o = acc[:, :LANES] / acc[:, LANES:]
        o = o[:tq] - lam * o[tq:]
        out_ref[0, 0, pl.ds(q_start, tq), :] = _rms_rows(o, sg_ref[...]) * (1.0 - lambda_init)
        return tile_carry

    lax.fori_loop(0, seq // tq, q_tile, 0)


def _diff_flash(q, k, v, lq1, lk1, lq2, lk2, subln_g, lambda_init, tq, tk):
    B, T, C = q.shape
    H = C // LANES
    slopes = jnp.exp2(-8.0 * jnp.arange(1, H + 1, dtype=F32) / H) * LOG2_E
    assert T // FEAT_RADIX <= 256
    s1 = slopes.astype(BF16)
    r1 = slopes - s1.astype(F32)
    s2 = r1.astype(BF16)
    s3 = (r1 - s2.astype(F32)).astype(BF16)
    pieces = jnp.stack([s1, s2, s3], axis=1).astype(F32)
    qfeat = jnp.concatenate([pieces * FEAT_RADIX, pieces, jnp.zeros((H, LANES - 6), F32)], axis=1)
    qfeat = qfeat.astype(BF16).reshape(H, 1, LANES)
    pos = jnp.arange(T)
    hi = jnp.broadcast_to((pos // FEAT_RADIX)[:, None], (T, 3))
    lo = jnp.broadcast_to((pos % FEAT_RADIX)[:, None], (T, 3))
    kfeat = jnp.concatenate([hi, lo, jnp.zeros((T, LANES - 6), jnp.int32)], axis=1).astype(BF16)
    ones = jnp.ones((tk // 2, LANES), BF16)
    lvec = _const_spec((1, HEAD_DIM))
    seq = pl.BlockSpec((1, T, LANES), lambda b, h: (b, 0, h))
    return pl.pallas_call(
        functools.partial(_diff_flash_kernel, lambda_init=lambda_init, tq=tq, tk=tk),
        grid=(B, H),
        in_specs=[seq, seq, seq,
                  pl.BlockSpec((1, 1, LANES), lambda b, h: (h, 0, 0)), _const_spec((T, LANES)),
                  _const_spec((tk // 2, LANES)),
                  lvec, lvec, lvec, lvec, _const_spec((1, LANES))],
        out_specs=pl.BlockSpec((1, 1, T, LANES), lambda b, h: (b, h, 0, 0)),
        out_shape=jax.ShapeDtypeStruct((B, H, T, LANES), F32),
        scratch_shapes=[pltpu.VMEM((2 * tq, tk // 2), F32)] * 2,
        compiler_params=_params("arbitrary", "arbitrary"),
        name="diff_flash",
    )(q, k, v, qfeat, kfeat, ones,
      lq1.reshape(1, -1), lk1.reshape(1, -1), lq2.reshape(1, -1), lk2.reshape(1, -1),
      subln_g.reshape(1, LANES))


def _attn_out_kernel(mix_ref, qm_ref, km_ref, vm_ref, x_ref, qg_ref, kg_ref, wa_ref, wb_ref, gmean_ref,
                     out_ref):
    qm = qm_ref[0]
    ms = _dot((qm * qm).astype(BF16), gmean_ref[...])
    qn = qm * lax.rsqrt(ms + RMS_EPS) * qg_ref[...]
    kb = (km_ref[0] * kg_ref[...]).astype(BF16)
    vb = vm_ref[0].astype(BF16)
    lane = lax.broadcasted_iota(jnp.int32, qn.shape, 1)
    mem = jnp.zeros_like(qn)
    for hd in range(MEM_HEADS):
        in_head = (lane >= hd * HEAD_DIM) & (lane < (hd + 1) * HEAD_DIM)
        s = _dot_nt(jnp.where(in_head, qn, 0.0).astype(BF16), kb)
        p = jnp.exp(s - jnp.max(s, axis=-1, keepdims=True))
        o = _dot(p.astype(BF16), vb) / jnp.sum(p, axis=-1, keepdims=True)
        mem = jnp.where(in_head, o, mem)
    mix = jnp.concatenate([mix_ref[0, hp] for hp in range(HEAD_PAIRS)], axis=1)
    y = _dot(mix.astype(BF16), wa_ref[...]) + _dot(mem.astype(BF16), wb_ref[...])
    out_ref[0] = x_ref[0] + y


def _attn_out(mix, qm, k_mem, v_mem, x, q_g, k_g, w_out, tm):
    B, T, D = x.shape
    M = k_mem.shape[1]
    gmean = _group_matrix(MEM_WIDTH, HEAD_DIM, 1.0 / HEAD_DIM)
    qg = jnp.tile(q_g, MEM_HEADS).reshape(1, MEM_WIDTH) * HEAD_DIM ** -0.5
    kg = jnp.tile(k_g, MEM_HEADS).reshape(1, MEM_WIDTH)
    w = w_out.astype(BF16)
    rows = lambda n: pl.BlockSpec((1, tm, n), lambda b, t: (b, t, 0))
    memspec = pl.BlockSpec((1, M, MEM_WIDTH), lambda b, t: (b, 0, 0))
    return pl.pallas_call(
        _attn_out_kernel,
        grid=(B, T // tm),
        in_specs=[_slab_spec(tm), rows(MEM_WIDTH), memspec, memspec, rows(D),
                  _const_spec((1, MEM_WIDTH)), _const_spec((1, MEM_WIDTH)),
                  _const_spec((MIX_WIDTH, D)), _const_spec((MEM_WIDTH, D)),
                  _const_spec((MEM_WIDTH, MEM_WIDTH))],
        out_specs=rows(D),
        out_shape=jax.ShapeDtypeStruct((B, T, D), F32),
        compiler_params=_params("arbitrary", "arbitrary"),
        name="attn_out",
    )(mix, qm, k_mem, v_mem, x, qg, kg, w[:MIX_WIDTH], w[MIX_WIDTH:], gmean)


def _ffn_kernel(x_ref, ng_ref, w1_ref, w2_ref, out_ref, *, ff_chunk):
    x = x_ref[0]
    h = _rms_rows(x, ng_ref[...]).astype(BF16)
    acc = x
    for c in range(0, w1_ref.shape[1], ff_chunk):
        u = jnp.maximum(_dot(h, w1_ref[:, c:c + ff_chunk]), 0.0)
        acc = acc + _dot((u * u).astype(BF16), w2_ref[c:c + ff_chunk, :])
    out_ref[0] = acc


def _ffn(x, norm_g, w1, w2, tm):
    B, T, D = x.shape
    F = w1.shape[1]
    rows = pl.BlockSpec((1, tm, D), lambda b, t: (b, t, 0))
    return pl.pallas_call(
        functools.partial(_ffn_kernel, ff_chunk=D),
        grid=(B, T // tm),
        in_specs=[rows, _const_spec((1, D)), _const_spec((D, F)), _const_spec((F, D))],
        out_specs=rows,
        out_shape=jax.ShapeDtypeStruct((B, T, D), F32),
        compiler_params=_params("arbitrary", "arbitrary"),
        name="ffn",
    )(x, norm_g.reshape(1, D), w1.astype(BF16), w2.astype(BF16))


def _diff_lambda_init(layer):
    return 0.8 - 0.6 * math.exp(-0.3 * layer)


def kernel(x, mem, norm_mix_g, norm_ffn_g, w_out, w_ff1, w_ff2, mem_norm_g, w_mem_kv, mem_q_norm_g, mem_k_norm_g, rw_in, rw_mu, rw_w0, rw_w2, rw_a0, rw_a2, rw_g2, rw_k_k, rw_k_a, rw_r_k, rw_lnx_g, rw_lnx_b, df_in, df_q_norm_g, df_k_norm_g, df_lq1, df_lk1, df_lq2, df_lk2, df_subln_g):
    T = x.shape[1]
    tm = min(T, 512)
    depth = norm_mix_g.shape[0]
    k_mem, v_mem = _mem_kv(mem, mem_norm_g, w_mem_kv)
    for layer in range(depth):
        j = layer // 2
        if layer % 2 == 0:
            r, k, v, kn, a, cs, bonus, g, qm = _rwkv_in(
                x, norm_mix_g[layer], rw_in[j], rw_mu[j], rw_w0[j], rw_w2[j], rw_a0[j], rw_a2[j],
                rw_g2[j], rw_k_k[j], rw_k_a[j], rw_r_k[j], tm)
            mix = _rwkv_scan(r, k, v, kn, a, cs, bonus, g, rw_lnx_g[j], rw_lnx_b[j], min(T, 512))
        else:
            q, k, v, qm = _diff_in(x, norm_mix_g[layer], df_in[j], df_q_norm_g[j], df_k_norm_g[j], tm)
            mix = _diff_flash(q, k, v, df_lq1[j], df_lk1[j], df_lq2[j], df_lk2[j], df_subln_g[j],
                              _diff_lambda_init(layer), min(T, 512), min(T, 1024))
        x = _attn_out(mix, qm, k_mem, v_mem, x, mem_q_norm_g[layer], mem_k_norm_g[layer], w_out[layer], tm)
        x = _ffn(x, norm_ffn_g[layer], w_ff1[layer], w_ff2[layer], min(T, 512))
    return x
```

```python
import functools
import math

import jax
import jax.numpy as jnp
from jax import lax
from jax.experimental import pallas as pl
from jax.experimental.pallas import tpu as pltpu

F32 = jnp.float32
BF16 = jnp.bfloat16

D_MODEL = 1024
HEAD_DIM = 64
MIX_WIDTH = 768
MEM_WIDTH = 256
MEM_HEADS = 4
DECAY_LORA = 64
ICLR_LORA = 64
GATE_LORA = 128
RWKV_COLS = 3 * MIX_WIDTH + DECAY_LORA + ICLR_LORA + GATE_LORA
D_FF = 4 * D_MODEL
RMS_EPS = 1e-6
GN_EPS = 64e-5

LANES = 128
HEAD_PAIRS = MIX_WIDTH // LANES
GROUP_COLS = 256
CHUNK = 64
SCAN_PAIRS = 3
NEG_BIG = -1e30
LOG2_E = math.log2(math.e)
FEAT_RADIX = 32
ONES_ROWS = 16
VMEM_LIMIT = 56 * 1024 * 1024


def _dot(a, b):
    return jnp.dot(a, b, preferred_element_type=F32)


def _dot_nt(a, b):
    return lax.dot_general(a, b, (((1,), (1,)), ((), ())), preferred_element_type=F32)


def _dot_split(x, m):
    hi = x.astype(BF16)
    lo = (x - hi.astype(F32)).astype(BF16)
    return _dot(hi, m) + _dot(lo, m)


def _dot_split3(m, x):
    hi = x.astype(BF16)
    r1 = x - hi.astype(F32)
    mid = r1.astype(BF16)
    lo = (r1 - mid.astype(F32)).astype(BF16)
    return _dot(m, hi) + _dot(m, mid) + _dot(m, lo)


def _sigmoid(x):
    return 1.0 / (1.0 + jnp.exp(-x))


def _rms_rows(x, g):
    return x * lax.rsqrt(jnp.mean(x * x, axis=-1, keepdims=True) + RMS_EPS) * g


def _group_matrix(n, group, value):
    i = jnp.arange(n) // group
    return jnp.where(i[:, None] == i[None, :], value, 0.0).astype(BF16)


def _params(*sem):
    return pltpu.CompilerParams(dimension_semantics=sem, vmem_limit_bytes=VMEM_LIMIT)


def _const_spec(shape):
    nd = len(shape)
    return pl.BlockSpec(shape, lambda *_: (0,) * nd)


def _slab_spec(tm):
    return pl.BlockSpec((1, HEAD_PAIRS, tm, LANES), lambda b, t: (b, 0, t, 0))


def _mem_kv_kernel(mem_ref, g_ref, w_ref, gm_ref, k_out, v_out):
    h = _rms_rows(mem_ref[0], g_ref[...])
    kv = _dot(h.astype(BF16), w_ref[...])
    k = kv[:, :MEM_WIDTH]
    ms = _dot_split(k * k, gm_ref[...])
    k_out[0] = k * lax.rsqrt(ms + RMS_EPS)
    v_out[0] = kv[:, MEM_WIDTH:]


def _mem_kv(mem, mem_norm_g, w_mem_kv):
    B, M, D = mem.shape
    gm = _group_matrix(MEM_WIDTH, HEAD_DIM, 1.0 / HEAD_DIM)
    out = jax.ShapeDtypeStruct((B, M, MEM_WIDTH), F32)
    return pl.pallas_call(
        _mem_kv_kernel,
        grid=(B,),
        in_specs=[pl.BlockSpec((1, M, D), lambda b: (b, 0, 0)),
                  _const_spec((1, D)), _const_spec((D, 2 * MEM_WIDTH)),
                  _const_spec((MEM_WIDTH, MEM_WIDTH))],
        out_specs=[pl.BlockSpec((1, M, MEM_WIDTH), lambda b: (b, 0, 0))] * 2,
        out_shape=[out, out],
        compiler_params=_params("arbitrary"),
        name="mem_kv",
    )(mem, mem_norm_g.reshape(1, D), w_mem_kv.astype(BF16), gm)


def _rwkv_in_kernel(x_ref, ng_ref, w_ref, mu_ref, w0_ref, w2_ref, a0_ref, a2_ref, g2_ref,
                    kk_ref, ka_ref, rk_ref, gsum_ref, tri_ref,
                    r_out, k_out, v_out, kn_out, a_out, cs_out, bonus_out, g_out, qm_out,
                    carry_ref):
    C = MIX_WIDTH
    tm = x_ref.shape[1]

    @pl.when(pl.program_id(1) == 0)
    def _():
        carry_ref[...] = jnp.zeros_like(carry_ref)

    h = _rms_rows(x_ref[0], ng_ref[...]).astype(BF16)
    proj = _dot(h, w_ref[...])
    cur = proj[:, :RWKV_COLS]
    row = lax.broadcasted_iota(jnp.int32, (tm, 1), 0)
    prev = jnp.where(row == 0, carry_ref[0:1, :], pltpu.roll(cur, 1, axis=0))
    carry_ref[0:1, :] = cur[tm - 1:tm, :]
    slab = cur + (prev - cur) * mu_ref[...]

    r = slab[:, 0:C]
    k = slab[:, C:2 * C]
    v = slab[:, 2 * C:3 * C]
    wa = slab[:, 3 * C:3 * C + DECAY_LORA + ICLR_LORA]
    gd = slab[:, 3 * C + DECAY_LORA + ICLR_LORA:RWKV_COLS]

    d = w0_ref[...] + _dot(jnp.tanh(wa).astype(BF16), w2_ref[...])
    log_decay = -math.exp(-0.5) * _sigmoid(d)
    a = _sigmoid(a0_ref[...] + _dot(wa.astype(BF16), a2_ref[...]))
    g = _dot(_sigmoid(gd).astype(BF16), g2_ref[...])

    gsum = gsum_ref[...]

    def head_sum(t):
        return jnp.concatenate(
            [_dot(t[:, i:i + GROUP_COLS].astype(BF16), gsum) for i in range(0, C, GROUP_COLS)], axis=1)

    kn = k * kk_ref[...]
    kn = kn * lax.rsqrt(jnp.maximum(head_sum(kn * kn), 1e-24))
    k2 = k * (1.0 + (a - 1.0) * ka_ref[...])
    bonus = head_sum(r * k2 * rk_ref[...]) * v

    cs = jnp.concatenate([_dot_split3(tri_ref[...], log_decay[i:i + GROUP_COLS])
                          for i in range(0, tm, GROUP_COLS)], axis=0)
    outs = ((r_out, r), (k_out, k2), (v_out, v), (kn_out, kn), (a_out, a), (cs_out, cs),
            (bonus_out, bonus), (g_out, g))
    for ref, val in outs:
        for hp in range(HEAD_PAIRS):
            ref[0, hp] = val[:, hp * LANES:(hp + 1) * LANES]
    qm_out[0] = proj[:, RWKV_COLS:]


def _rwkv_in(x, norm_g, w_in, mu, w0, w2, a0, a2, g2, k_k, k_a, r_k, tm):
    B, T, D = x.shape
    C = MIX_WIDTH
    n_in = w_in.shape[1]
    zeros = jnp.zeros((DECAY_LORA, C), F32)
    w2p = jnp.concatenate([w2, zeros], axis=0).astype(BF16)
    a2p = jnp.concatenate([zeros, a2], axis=0).astype(BF16)
    gsum = _group_matrix(GROUP_COLS, HEAD_DIM, 1.0)
    t_idx = jnp.arange(GROUP_COLS)
    tri = ((t_idx[:, None] // CHUNK == t_idx[None, :] // CHUNK)
           & (t_idx[None, :] <= t_idx[:, None])).astype(BF16)
    row = lambda p: p.reshape(1, -1)
    big = jax.ShapeDtypeStruct((B, HEAD_PAIRS, T, LANES), F32)
    return pl.pallas_call(
        _rwkv_in_kernel,
        grid=(B, T // tm),
        in_specs=[pl.BlockSpec((1, tm, D), lambda b, t: (b, t, 0)),
                  _const_spec((1, D)), _const_spec((D, n_in)), _const_spec((1, RWKV_COLS)),
                  _const_spec((1, C)), _const_spec((2 * DECAY_LORA, C)),
                  _const_spec((1, C)), _const_spec((2 * ICLR_LORA, C)),
                  _const_spec((GATE_LORA, C)),
                  _const_spec((1, C)), _const_spec((1, C)), _const_spec((1, C)),
                  _const_spec((GROUP_COLS, GROUP_COLS)), _const_spec((GROUP_COLS, GROUP_COLS))],
        out_specs=[_slab_spec(tm)] * 8 + [pl.BlockSpec((1, tm, MEM_WIDTH), lambda b, t: (b, t, 0))],
        out_shape=[big] * 8 + [jax.ShapeDtypeStruct((B, T, MEM_WIDTH), F32)],
        scratch_shapes=[pltpu.VMEM((8, RWKV_COLS), F32)],
        compiler_params=_params("arbitrary", "arbitrary"),
        name="rwkv_in",
    )(x, row(norm_g), w_in.astype(BF16), row(mu), row(w0), w2p, row(a0), a2p, g2.astype(BF16),
      row(k_k), row(k_a), row(r_k), gsum, tri)


def _rwkv_scan_kernel(r_ref, k_ref, v_ref, kn_ref, a_ref, cs_ref, bonus_ref, g_ref,
                      lng_ref, lnb_ref, gmean_ref, out_ref,
                      s_ref, rhat_ref, y_ref, m_ref, c_ref):
    L = CHUNK
    tc = r_ref.shape[2]
    n_chunks = tc // L

    @pl.when(pl.program_id(1) == 0)
    def _():
        s_ref[...] = jnp.zeros_like(s_ref)

    lane = lax.broadcasted_iota(jnp.int32, (L, LANES), 1)
    row = lax.broadcasted_iota(jnp.int32, (L, LANES), 0)
    head0 = lane < HEAD_DIM
    src = lane & (HEAD_DIM - 1)
    strict = row > src
    incl = row >= src
    r2 = lax.broadcasted_iota(jnp.int32, (LANES, LANES), 0)
    c2 = lax.broadcasted_iota(jnp.int32, (LANES, LANES), 1)
    same_head = (r2 >> 6) == (c2 >> 6)
    eye = r2 == c2
    eye_pk = jnp.where(row == src, 1.0, 0.0)
    first_row = (lax.broadcasted_iota(jnp.int32, (tc, LANES), 0) & (L - 1)) == 0

    def stack2(t):
        return jnp.concatenate([jnp.where(head0, t, 0.0), jnp.where(head0, 0.0, t)], axis=0)

    def side2(x, y):
        return jnp.concatenate([stack2(x), stack2(y)], axis=1).astype(BF16)

    def phase_a(trip, carry):
        hps = [trip * SCAN_PAIRS + i for i in range(SCAN_PAIRS)]
        cut = lambda t: [t[c * L:(c + 1) * L] for c in range(n_chunks)]
        rt, kt, at, bt, vv, w_last = [], [], [], [], [], []
        for hp in hps:
            kn = kn_ref[0, hp]
            cs = cs_ref[0, hp]
            cs_prev = jnp.where(first_row, 0.0, pltpu.roll(cs, 1, axis=0))
            w_t = jnp.exp(cs)
            w_inv = jnp.exp(-cs)
            rt += cut(r_ref[0, hp] * w_t)
            kt += cut(k_ref[0, hp] * w_inv)
            at += cut(-kn * jnp.exp(cs_prev))
            bt += cut(kn * a_ref[0, hp] * w_inv)
            vv += cut(v_ref[0, hp])
            w_last += [w_t[(c + 1) * L - 1:(c + 1) * L] for c in range(n_chunks)]
        chunks = range(SCAN_PAIRS * n_chunks)

        sc = [_dot_nt(jnp.concatenate([at[c], rt[c]], axis=0).astype(BF16),
                      jnp.concatenate([stack2(bt[c]), stack2(kt[c])], axis=0).astype(BF16))
              for c in chunks]
        a_ab = [jnp.where(strict, s[:L, :LANES], 0.0) for s in sc]
        a_ak = [jnp.where(strict, s[:L, LANES:], 0.0).astype(BF16) for s in sc]
        a_rb = [jnp.where(incl, s[L:, :LANES], 0.0).astype(BF16) for s in sc]
        a_rk = [jnp.where(incl, s[L:, LANES:], 0.0).astype(BF16) for s in sc]

        npow = a_ab
        inv = [eye_pk + n for n in npow]
        nbd = [stack2(n).astype(BF16) for n in npow]
        npow = [_dot(n.astype(BF16), b) for n, b in zip(npow, nbd)]
        for _ in range(int(math.log2(L)) - 2):
            nbd = [stack2(n).astype(BF16) for n in npow]
            both = [_dot(jnp.concatenate([n, i], axis=0).astype(BF16), b)
                    for n, i, b in zip(npow, inv, nbd)]
            npow = [t[:L] for t in both]
            inv = [i + t[L:] for i, t in zip(inv, both)]
        nbd = [stack2(n).astype(BF16) for n in npow]
        t_pk = [(i + _dot(i.astype(BF16), b)).astype(BF16) for i, b in zip(inv, nbd)]

        v2 = [stack2(t).astype(BF16) for t in vv]
        av = [_dot(jnp.concatenate([a_ak[c], a_rk[c]], axis=0), v2[c]) for c in chunks]
        q0 = [t[:L] for t in av]
        arkv = [t[L:] for t in av]
        pq = [_dot(t_pk[c], side2(at[c], q0[c])) for c in chunks]
        ar = [_dot(a_rb[c], side2(pq[c][:, :LANES], pq[c][:, LANES:])) for c in chunks]
        mc = [_dot(jnp.concatenate([bt[c] * w_last[c], kt[c] * w_last[c]], axis=0).T.astype(BF16),
                   jnp.concatenate(
                       [pq[c], jnp.concatenate([jnp.zeros_like(vv[c]), vv[c]], axis=1)], axis=0).astype(BF16))
              for c in chunks]
        for i in chunks:
            hp, c = hps[i // n_chunks], i % n_chunks
            rows = pl.ds(c * L, L)
            rhat_ref[hp, rows, :] = (rt[i] + ar[i][:, :LANES]).astype(BF16)
            y_ref[hp, rows, :] = ar[i][:, LANES:] + arkv[i]
            m_ref[hp, c] = (jnp.where(same_head, mc[i][:, :LANES], 0.0)
                            + jnp.where(eye, w_last[i], 0.0)).astype(BF16)
            c_ref[hp, c] = jnp.where(same_head, mc[i][:, LANES:], 0.0)
        return carry

    lax.fori_loop(0, HEAD_PAIRS // SCAN_PAIRS, phase_a, 0)

    def phase_b(c, carry):
        rows = pl.ds(pl.multiple_of(c * L, L), L)
        pairs = range(HEAD_PAIRS)
        sb = [s_ref[hp].astype(BF16) for hp in pairs]
        y = [_dot(rhat_ref[hp, rows, :], sb[hp]) for hp in pairs]
        s_new = [_dot(m_ref[hp, c], sb[hp]) for hp in pairs]
        for hp in pairs:
            y_ref[hp, rows, :] = y_ref[hp, rows, :] + y[hp]
            s_ref[hp] = s_new[hp] + c_ref[hp, c]
        return carry

    lax.fori_loop(0, n_chunks, phase_b, 0)

    gmean2 = gmean_ref[...]

    def head_mean(t):
        hi = t.astype(BF16)
        lo = (t - hi.astype(F32)).astype(BF16)
        return _dot(jnp.concatenate([hi, lo], axis=1), gmean2)

    pairs = range(HEAD_PAIRS)
    ys = [y_ref[hp] for hp in pairs]
    ds = [y - head_mean(y) for y in ys]
    var = [head_mean(d * d) for d in ds]
    for hp in pairs:
        y = ds[hp] * lax.rsqrt(var[hp] + GN_EPS) * lng_ref[hp] + lnb_ref[hp]
        out_ref[0, hp] = (y + bonus_ref[0, hp]) * g_ref[0, hp]


def _rwkv_scan(r, k, v, kn, a, cs, bonus, g, lnx_g, lnx_b, tc):
    B, HP, T, _ = r.shape
    gmean = _group_matrix(LANES, HEAD_DIM, 1.0 / HEAD_DIM)
    gmean = jnp.concatenate([gmean, gmean], axis=0)
    n_chunks = tc // CHUNK
    return pl.pallas_call(
        _rwkv_scan_kernel,
        grid=(B, T // tc),
        in_specs=[_slab_spec(tc)] * 8 + [_const_spec((HP, 1, LANES))] * 2 + [_const_spec((2 * LANES, LANES))],
        out_specs=_slab_spec(tc),
        out_shape=jax.ShapeDtypeStruct((B, HP, T, LANES), F32),
        scratch_shapes=[pltpu.VMEM((HP, LANES, LANES), F32),
                        pltpu.VMEM((HP, tc, LANES), BF16),
                        pltpu.VMEM((HP, tc, LANES), F32),
                        pltpu.VMEM((HP, n_chunks, LANES, LANES), BF16),
                        pltpu.VMEM((HP, n_chunks, LANES, LANES), F32)],
        compiler_params=_params("arbitrary", "arbitrary"),
        name="rwkv_scan",
    )(r, k, v, kn, a, cs, bonus, g, lnx_g.reshape(HP, 1, LANES), lnx_b.reshape(HP, 1, LANES), gmean)


def _diff_in_kernel(x_ref, ng_ref, w_ref, qg_ref, kg_ref, gmean_ref, q_out, k_out, v_out, qm_out):
    C = MIX_WIDTH
    h = _rms_rows(x_ref[0], ng_ref[...]).astype(BF16)
    proj = _dot(h, w_ref[...])
    gmean = gmean_ref[...]

    def head_norm(t):
        ms = jnp.concatenate(
            [_dot(jnp.square(t[:, i:i + GROUP_COLS]).astype(BF16), gmean)
             for i in range(0, C, GROUP_COLS)], axis=1)
        return t * lax.rsqrt(ms + RMS_EPS)

    q_out[0] = (head_norm(proj[:, :C]) * qg_ref[...]).astype(BF16)
    k_out[0] = (head_norm(proj[:, C:2 * C]) * kg_ref[...]).astype(BF16)
    v_out[0] = proj[:, 2 * C:3 * C].T.astype(BF16)
    qm_out[0] = proj[:, 3 * C:]


def _diff_in(x, norm_g, w_in, q_g, k_g, tm):
    B, T, D = x.shape
    C = MIX_WIDTH
    n_in = w_in.shape[1]
    gmean = _group_matrix(GROUP_COLS, HEAD_DIM, 1.0 / HEAD_DIM)
    qg = jnp.tile(q_g.reshape(-1), HEAD_PAIRS).reshape(1, C) * (HEAD_DIM ** -0.5 * LOG2_E)
    kg = jnp.tile(k_g.reshape(-1), HEAD_PAIRS).reshape(1, C)
    tile = pl.BlockSpec((1, tm, C), lambda b, t: (b, t, 0))
    big = jax.ShapeDtypeStruct((B, T, C), BF16)
    return pl.pallas_call(
        _diff_in_kernel,
        grid=(B, T // tm),
        in_specs=[pl.BlockSpec((1, tm, D), lambda b, t: (b, t, 0)),
                  _const_spec((1, D)), _const_spec((D, n_in)),
                  _const_spec((1, C)), _const_spec((1, C)), _const_spec((GROUP_COLS, GROUP_COLS))],
        out_specs=[tile, tile, pl.BlockSpec((1, C, tm), lambda b, t: (b, 0, t)),
                   pl.BlockSpec((1, tm, MEM_WIDTH), lambda b, t: (b, t, 0))],
        out_shape=[big, big, jax.ShapeDtypeStruct((B, C, T), BF16),
                   jax.ShapeDtypeStruct((B, T, MEM_WIDTH), F32)],
        compiler_params=_params("arbitrary", "arbitrary"),
        name="diff_in",
    )(x, norm_g.reshape(1, D), w_in.astype(BF16), qg, kg, gmean)


def _diff_flash_kernel(q_ref, k_ref, vt_ref, qfeat_ref, kfeat_ref,
                       lq1_ref, lk1_ref, lq2_ref, lk2_ref, sg_ref,
                       out_ref, sa_ref, sb_ref, *, lambda_init, tq, tk):
    seq = q_ref.shape[1]
    lam = (jnp.exp(jnp.sum(lq1_ref[...] * lk1_ref[...], axis=-1, keepdims=True))
           - jnp.exp(jnp.sum(lq2_ref[...] * lk2_ref[...], axis=-1, keepdims=True)) + lambda_init)
    th = tk // 2
    dim = lax.broadcasted_iota(jnp.int32, (LANES, tq), 0)
    key_loc = lax.broadcasted_iota(jnp.int32, (th, 1), 0)
    q_loc = lax.broadcasted_iota(jnp.int32, (1, 2 * tq), 1) & (tq - 1)
    qfeat = jnp.concatenate([qfeat_ref[0]] * (2 * tq // LANES), axis=1)
    ones_rows = jnp.ones((ONES_ROWS, th), BF16)

    def stacked_queries(q_start):
        qt = q_ref[0, pl.ds(pl.multiple_of(q_start, tq), tq), :].astype(F32).T
        zero = jnp.zeros_like(qt)
        top = jnp.concatenate(
            [jnp.where(dim < HEAD_DIM, qt, zero), jnp.where(dim < HEAD_DIM, zero, qt)], axis=1)
        return jnp.concatenate([top.astype(BF16), qfeat], axis=0)

    def tile_scores(qs, k_start):
        rows = pl.ds(pl.multiple_of(k_start, th), th)
        return _dot(jnp.concatenate([k_ref[0, rows, :], kfeat_ref[rows, :]], axis=1), qs)

    sa_ref[...] = tile_scores(stacked_queries(0), 0)

    def q_tile(qi, tile_carry):
        q_start = pl.multiple_of(qi * tq, tq)
        qs = stacked_queries(q_start)
        scores = functools.partial(tile_scores, qs)

        def softmax_pv(s, k_start, carry, masked):
            m, acc = carry
            if masked:
                s = jnp.where(key_loc + (k_start - q_start) <= q_loc, s, NEG_BIG)
            m_new = jnp.maximum(m, jnp.max(s, axis=0, keepdims=True))
            p = jnp.exp2(s - m_new).astype(BF16)
            vt = vt_ref[0, :, pl.ds(pl.multiple_of(k_start, th), th)]
            pv = _dot(jnp.concatenate([vt, ones_rows], axis=0), p)
            return m_new, jnp.exp2(m - m_new) * acc + pv

        def full_block(j, carry):
            k0 = j * tk
            sb_ref[...] = scores(k0 + th)
            carry = softmax_pv(sa_ref[...], k0, carry, False)
            sa_ref[...] = scores(k0 + tk)
            return softmax_pv(sb_ref[...], k0 + th, carry, False)

        n_full = q_start // tk
        carry = (jnp.full((1, 2 * tq), NEG_BIG, F32), jnp.zeros((LANES + ONES_ROWS, 2 * tq), F32))
        carry = lax.fori_loop(0, n_full, full_block, carry)
        k0 = n_full * tk
        second_half = q_start + tq > k0 + th

        @pl.when(second_half)
        def _():
            sb_ref[...] = scores(k0 + th)

        carry = softmax_pv(sa_ref[...], k0, carry, True)
        sa_ref[...] = tile_scores(stacked_queries(jnp.minimum(q_start + tq, seq - tq)), 0)
        _, acc = lax.cond(second_half,
                          lambda c: softmax_pv(sb_ref[...], k0 + th, c, True),
                          lambda c: c, carry)

        o = acc[:LANES] / acc[LANES:LANES + 1]
        o = o[:, :tq] - lam * o[:, tq:]
        o = o * lax.rsqrt(jnp.mean(o * o, axis=0, keepdims=True) + RMS_EPS) * sg_ref[...]
        out_ref[0, 0, pl.ds(q_start, tq), :] = (o * (1.0 - lambda_init)).T
        return tile_carry

    lax.fori_loop(0, seq // tq, q_tile, 0)


def _diff_flash(q, k, vt, lq1, lk1, lq2, lk2, subln_g, lambda_init, tq, tk):
    B, T, C = q.shape
    H = C // LANES
    slopes = jnp.exp2(-8.0 * jnp.arange(1, H + 1, dtype=F32) / H) * LOG2_E
    assert T // FEAT_RADIX <= 256
    s1 = slopes.astype(BF16)
    r1 = slopes - s1.astype(F32)
    s2 = r1.astype(BF16)
    s3 = (r1 - s2.astype(F32)).astype(BF16)
    pieces = jnp.stack([s1, s2, s3], axis=1).astype(F32)
    qfeat = jnp.concatenate([pieces * FEAT_RADIX, pieces, jnp.zeros((H, LANES - 6), F32)], axis=1)
    qfeat = jnp.broadcast_to(qfeat.astype(BF16)[:, :, None], (H, LANES, LANES))
    pos = jnp.arange(T)
    hi = jnp.broadcast_to((pos // FEAT_RADIX)[:, None], (T, 3))
    lo = jnp.broadcast_to((pos % FEAT_RADIX)[:, None], (T, 3))
    kfeat = jnp.concatenate([hi, lo, jnp.zeros((T, LANES - 6), jnp.int32)], axis=1).astype(BF16)
    lvec = _const_spec((1, HEAD_DIM))
    seq = pl.BlockSpec((1, T, LANES), lambda b, h: (b, 0, h))
    return pl.pallas_call(
        functools.partial(_diff_flash_kernel, lambda_init=lambda_init, tq=tq, tk=tk),
        grid=(B, H),
        in_specs=[seq, seq, pl.BlockSpec((1, LANES, T), lambda b, h: (b, h, 0)),
                  pl.BlockSpec((1, LANES, LANES), lambda b, h: (h, 0, 0)), _const_spec((T, LANES)),
                  lvec, lvec, lvec, lvec, _const_spec((LANES, 1))],
        out_specs=pl.BlockSpec((1, 1, T, LANES), lambda b, h: (b, h, 0, 0)),
        out_shape=jax.ShapeDtypeStruct((B, H, T, LANES), F32),
        scratch_shapes=[pltpu.VMEM((tk // 2, 2 * tq), F32)] * 2,
        compiler_params=_params("arbitrary", "arbitrary"),
        name="diff_flash",
    )(q, k, vt, qfeat, kfeat,
      lq1.reshape(1, -1), lk1.reshape(1, -1), lq2.reshape(1, -1), lk2.reshape(1, -1),
      subln_g.reshape(LANES, 1))


def _attn_out_kernel(mix_ref, qm_ref, km_ref, vm_ref, x_ref, qg_ref, kg_ref, wa_ref, wb_ref, gmean_ref,
                     out_ref):
    qm = qm_ref[0]
    ms = _dot((qm * qm).astype(BF16), gmean_ref[...])
    qn = qm * lax.rsqrt(ms + RMS_EPS) * qg_ref[...]
    kb = (km_ref[0] * kg_ref[...]).astype(BF16)
    vb = vm_ref[0].astype(BF16)
    lane = lax.broadcasted_iota(jnp.int32, qn.shape, 1)
    mem = jnp.zeros_like(qn)
    for hd in range(MEM_HEADS):
        in_head = (lane >= hd * HEAD_DIM) & (lane < (hd + 1) * HEAD_DIM)
        s = _dot_nt(jnp.where(in_head, qn, 0.0).astype(BF16), kb)
        p = jnp.exp(s - jnp.max(s, axis=-1, keepdims=True))
        o = _dot(p.astype(BF16), vb) / jnp.sum(p, axis=-1, keepdims=True)
        mem = jnp.where(in_head, o, mem)
    mix = jnp.concatenate([mix_ref[0, hp] for hp in range(HEAD_PAIRS)], axis=1)
    y = _dot(mix.astype(BF16), wa_ref[...]) + _dot(mem.astype(BF16), wb_ref[...])
    out_ref[0] = x_ref[0] + y


def _attn_out(mix, qm, k_mem, v_mem, x, q_g, k_g, w_out, tm):
    B, T, D = x.shape
    M = k_mem.shape[1]
    gmean = _group_matrix(MEM_WIDTH, HEAD_DIM, 1.0 / HEAD_DIM)
    qg = jnp.tile(q_g, MEM_HEADS).reshape(1, MEM_WIDTH) * HEAD_DIM ** -0.5
    kg = jnp.tile(k_g, MEM_HEADS).reshape(1, MEM_WIDTH)
    w = w_out.astype(BF16)
    rows = lambda n: pl.BlockSpec((1, tm, n), lambda b, t: (b, t, 0))
    memspec = pl.BlockSpec((1, M, MEM_WIDTH), lambda b, t: (b, 0, 0))
    return pl.pallas_call(
        _attn_out_kernel,
        grid=(B, T // tm),
        in_specs=[_slab_spec(tm), rows(MEM_WIDTH), memspec, memspec, rows(D),
                  _const_spec((1, MEM_WIDTH)), _const_spec((1, MEM_WIDTH)),
                  _const_spec((MIX_WIDTH, D)), _const_spec((MEM_WIDTH, D)),
                  _const_spec((MEM_WIDTH, MEM_WIDTH))],
        out_specs=rows(D),
        out_shape=jax.ShapeDtypeStruct((B, T, D), F32),
        compiler_params=_params("arbitrary", "arbitrary"),
        name="attn_out",
    )(mix, qm, k_mem, v_mem, x, qg, kg, w[:MIX_WIDTH], w[MIX_WIDTH:], gmean)


def _ffn_kernel(x_ref, ng_ref, w1_ref, w2_ref, out_ref, *, ff_chunk):
    x = x_ref[0]
    h = _rms_rows(x, ng_ref[...]).astype(BF16)
    acc = x
    for c in range(0, w1_ref.shape[1], ff_chunk):
        u = jnp.maximum(_dot(h, w1_ref[:, c:c + ff_chunk]), 0.0)
        acc = acc + _dot((u * u).astype(BF16), w2_ref[c:c + ff_chunk, :])
    out_ref[0] = acc


def _ffn(x, norm_g, w1, w2, tm):
    B, T, D = x.shape
    F = w1.shape[1]
    rows = pl.BlockSpec((1, tm, D), lambda b, t: (b, t, 0))
    return pl.pallas_call(
        functools.partial(_ffn_kernel, ff_chunk=D),
        grid=(B, T // tm),
        in_specs=[rows, _const_spec((1, D)), _const_spec((D, F)), _const_spec((F, D))],
        out_specs=rows,
        out_shape=jax.ShapeDtypeStruct((B, T, D), F32),
        compiler_params=_params("arbitrary", "arbitrary"),
        name="ffn",
    )(x, norm_g.reshape(1, D), w1.astype(BF16), w2.astype(BF16))


def _diff_lambda_init(layer):
    return 0.8 - 0.6 * math.exp(-0.3 * layer)


def kernel(x, mem, norm_mix_g, norm_ffn_g, w_out, w_ff1, w_ff2, mem_norm_g, w_mem_kv, mem_q_norm_g, mem_k_norm_g, rw_in, rw_mu, rw_w0, rw_w2, rw_a0, rw_a2, rw_g2, rw_k_k, rw_k_a, rw_r_k, rw_lnx_g, rw_lnx_b, df_in, df_q_norm_g, df_k_norm_g, df_lq1, df_lk1, df_lq2, df_lk2, df_subln_g):
    T = x.shape[1]
    tm = min(T, 512)
    depth = norm_mix_g.shape[0]
    k_mem, v_mem = _mem_kv(mem, mem_norm_g, w_mem_kv)
    for layer in range(depth):
        j = layer // 2
        if layer % 2 == 0:
            r, k, v, kn, a, cs, bonus, g, qm = _rwkv_in(
                x, norm_mix_g[layer], rw_in[j], rw_mu[j], rw_w0[j], rw_w2[j], rw_a0[j], rw_a2[j],
                rw_g2[j], rw_k_k[j], rw_k_a[j], rw_r_k[j], tm)
            mix = _rwkv_scan(r, k, v, kn, a, cs, bonus, g, rw_lnx_g[j], rw_lnx_b[j], min(T, 512))
        else:
            q, k, v, qm = _diff_in(x, norm_mix_g[layer], df_in[j], df_q_norm_g[j], df_k_norm_g[j], tm)
            mix = _diff_flash(q, k, v, df_lq1[j], df_lk1[j], df_lq2[j], df_lk2[j], df_subln_g[j],
                              _diff_lambda_init(layer), min(T, 512), min(T, 1024))
        x = _attn_out(mix, qm, k_mem, v_mem, x, mem_q_norm_g[layer], mem_k_norm_g[layer], w_out[layer], tm)
        x = _ffn(x, norm_ffn_g[layer], w_ff1[layer], w_ff2[layer], min(T, 512))
    return x
```

```python
import functools
import math

import jax
import jax.numpy as jnp
from jax import lax
from jax.experimental import pallas as pl
from jax.experimental.pallas import tpu as pltpu

F32 = jnp.float32
BF16 = jnp.bfloat16

D_MODEL = 1024
HEAD_DIM = 64
MIX_WIDTH = 768
MEM_WIDTH = 256
MEM_HEADS = 4
DECAY_LORA = 64
ICLR_LORA = 64
GATE_LORA = 128
RWKV_COLS = 3 * MIX_WIDTH + DECAY_LORA + ICLR_LORA + GATE_LORA
D_FF = 4 * D_MODEL
RMS_EPS = 1e-6
GN_EPS = 64e-5

LANES = 128
HEAD_PAIRS = MIX_WIDTH // LANES
GROUP_COLS = 256
CHUNK = 64
SCAN_PAIRS = 3
NEG_BIG = -1e30
LOG2_E = math.log2(math.e)
FEAT_RADIX = 32
ONES_ROWS = 16
VMEM_LIMIT = 56 * 1024 * 1024


def _dot(a, b):
    return jnp.dot(a, b, preferred_element_type=F32)


def _dot_nt(a, b):
    return lax.dot_general(a, b, (((1,), (1,)), ((), ())), preferred_element_type=F32)


def _dot_split(x, m):
    hi = x.astype(BF16)
    lo = (x - hi.astype(F32)).astype(BF16)
    return _dot(hi, m) + _dot(lo, m)


def _dot_split_left(m, x):
    hi = x.astype(BF16)
    lo = (x - hi.astype(F32)).astype(BF16)
    return _dot(m, hi) + _dot(m, lo)


def _sigmoid(x):
    return 1.0 / (1.0 + jnp.exp(-x))


def _rms_rows(x, g):
    return x * lax.rsqrt(jnp.mean(x * x, axis=-1, keepdims=True) + RMS_EPS) * g


def _group_matrix(n, group, value):
    i = jnp.arange(n) // group
    return jnp.where(i[:, None] == i[None, :], value, 0.0).astype(BF16)


def _params(*sem):
    return pltpu.CompilerParams(dimension_semantics=sem, vmem_limit_bytes=VMEM_LIMIT)


def _const_spec(shape):
    nd = len(shape)
    return pl.BlockSpec(shape, lambda *_: (0,) * nd)


def _slab_spec(tm):
    return pl.BlockSpec((1, HEAD_PAIRS, tm, LANES), lambda b, t: (b, 0, t, 0))


def _mem_kv_kernel(mem_ref, g_ref, w_ref, gm_ref, k_out, v_out):
    h = _rms_rows(mem_ref[0], g_ref[...])
    kv = _dot(h.astype(BF16), w_ref[...])
    k = kv[:, :MEM_WIDTH]
    ms = _dot_split(k * k, gm_ref[...])
    k_out[0] = k * lax.rsqrt(ms + RMS_EPS)
    v_out[0] = kv[:, MEM_WIDTH:]


def _mem_kv(mem, mem_norm_g, w_mem_kv):
    B, M, D = mem.shape
    gm = _group_matrix(MEM_WIDTH, HEAD_DIM, 1.0 / HEAD_DIM)
    out = jax.ShapeDtypeStruct((B, M, MEM_WIDTH), F32)
    return pl.pallas_call(
        _mem_kv_kernel,
        grid=(B,),
        in_specs=[pl.BlockSpec((1, M, D), lambda b: (b, 0, 0)),
                  _const_spec((1, D)), _const_spec((D, 2 * MEM_WIDTH)),
                  _const_spec((MEM_WIDTH, MEM_WIDTH))],
        out_specs=[pl.BlockSpec((1, M, MEM_WIDTH), lambda b: (b, 0, 0))] * 2,
        out_shape=[out, out],
        compiler_params=_params("arbitrary"),
        name="mem_kv",
    )(mem, mem_norm_g.reshape(1, D), w_mem_kv.astype(BF16), gm)


def _rwkv_in_kernel(x_ref, ng_ref, w_ref, mu_ref, w0_ref, w2_ref, a0_ref, a2_ref, g2_ref,
                    kk_ref, ka_ref, rk_ref, gsum_ref, tri_ref,
                    r_out, k_out, v_out, kn_out, a_out, cs_out, bonus_out, g_out, qm_out,
                    carry_ref):
    C = MIX_WIDTH
    tm = x_ref.shape[1]

    @pl.when(pl.program_id(1) == 0)
    def _():
        carry_ref[...] = jnp.zeros_like(carry_ref)

    h = _rms_rows(x_ref[0], ng_ref[...]).astype(BF16)
    proj = _dot(h, w_ref[...])
    cur = proj[:, :RWKV_COLS]
    row = lax.broadcasted_iota(jnp.int32, (tm, 1), 0)
    prev = jnp.where(row == 0, carry_ref[0:1, :], pltpu.roll(cur, 1, axis=0))
    carry_ref[0:1, :] = cur[tm - 1:tm, :]
    slab = cur + (prev - cur) * mu_ref[...]

    r = slab[:, 0:C]
    k = slab[:, C:2 * C]
    v = slab[:, 2 * C:3 * C]
    wa = slab[:, 3 * C:3 * C + DECAY_LORA + ICLR_LORA]
    gd = slab[:, 3 * C + DECAY_LORA + ICLR_LORA:RWKV_COLS]

    d = w0_ref[...] + _dot(jnp.tanh(wa).astype(BF16), w2_ref[...])
    log_decay = -math.exp(-0.5) * _sigmoid(d)
    a = _sigmoid(a0_ref[...] + _dot(wa.astype(BF16), a2_ref[...]))
    g = _dot(_sigmoid(gd).astype(BF16), g2_ref[...])

    gsum = gsum_ref[...]

    def head_sum(t):
        return jnp.concatenate(
            [_dot(t[:, i:i + GROUP_COLS].astype(BF16), gsum) for i in range(0, C, GROUP_COLS)], axis=1)

    kn = k * kk_ref[...]
    kn = kn * lax.rsqrt(jnp.maximum(head_sum(kn * kn), 1e-24))
    k2 = k * (1.0 + (a - 1.0) * ka_ref[...])
    bonus = head_sum(r * k2 * rk_ref[...]) * v

    cs = jnp.concatenate([_dot_split_left(tri_ref[...], log_decay[i:i + GROUP_COLS])
                          for i in range(0, tm, GROUP_COLS)], axis=0)
    outs = ((r_out, r), (k_out, k2), (v_out, v), (kn_out, kn), (a_out, a), (cs_out, cs),
            (bonus_out, bonus), (g_out, g))
    for ref, val in outs:
        for hp in range(HEAD_PAIRS):
            ref[0, hp] = val[:, hp * LANES:(hp + 1) * LANES]
    qm_out[0] = proj[:, RWKV_COLS:]


def _rwkv_in(x, norm_g, w_in, mu, w0, w2, a0, a2, g2, k_k, k_a, r_k, tm):
    B, T, D = x.shape
    C = MIX_WIDTH
    n_in = w_in.shape[1]
    zeros = jnp.zeros((DECAY_LORA, C), F32)
    w2p = jnp.concatenate([w2, zeros], axis=0).astype(BF16)
    a2p = jnp.concatenate([zeros, a2], axis=0).astype(BF16)
    gsum = _group_matrix(GROUP_COLS, HEAD_DIM, 1.0)
    t_idx = jnp.arange(GROUP_COLS)
    tri = ((t_idx[:, None] // CHUNK == t_idx[None, :] // CHUNK)
           & (t_idx[None, :] <= t_idx[:, None])).astype(BF16)
    row = lambda p: p.reshape(1, -1)
    big = jax.ShapeDtypeStruct((B, HEAD_PAIRS, T, LANES), F32)
    return pl.pallas_call(
        _rwkv_in_kernel,
        grid=(B, T // tm),
        in_specs=[pl.BlockSpec((1, tm, D), lambda b, t: (b, t, 0)),
                  _const_spec((1, D)), _const_spec((D, n_in)), _const_spec((1, RWKV_COLS)),
                  _const_spec((1, C)), _const_spec((2 * DECAY_LORA, C)),
                  _const_spec((1, C)), _const_spec((2 * ICLR_LORA, C)),
                  _const_spec((GATE_LORA, C)),
                  _const_spec((1, C)), _const_spec((1, C)), _const_spec((1, C)),
                  _const_spec((GROUP_COLS, GROUP_COLS)), _const_spec((GROUP_COLS, GROUP_COLS))],
        out_specs=[_slab_spec(tm)] * 8 + [pl.BlockSpec((1, tm, MEM_WIDTH), lambda b, t: (b, t, 0))],
        out_shape=[big] * 8 + [jax.ShapeDtypeStruct((B, T, MEM_WIDTH), F32)],
        scratch_shapes=[pltpu.VMEM((8, RWKV_COLS), F32)],
        compiler_params=_params("arbitrary", "arbitrary"),
        name="rwkv_in",
    )(x, row(norm_g), w_in.astype(BF16), row(mu), row(w0), w2p, row(a0), a2p, g2.astype(BF16),
      row(k_k), row(k_a), row(r_k), gsum, tri)


def _rwkv_scan_kernel(r_ref, k_ref, v_ref, kn_ref, a_ref, cs_ref, bonus_ref, g_ref,
                      lng_ref, lnb_ref, gmean_ref, out_ref,
                      s_ref, rhat_ref, y_ref, m_ref, c_ref):
    L = CHUNK
    tc = r_ref.shape[2]
    n_chunks = tc // L

    @pl.when(pl.program_id(1) == 0)
    def _():
        s_ref[...] = jnp.zeros_like(s_ref)

    lane = lax.broadcasted_iota(jnp.int32, (L, LANES), 1)
    row = lax.broadcasted_iota(jnp.int32, (L, LANES), 0)
    head0 = lane < HEAD_DIM
    src = lane & (HEAD_DIM - 1)
    strict = row > src
    incl = row >= src
    r2 = lax.broadcasted_iota(jnp.int32, (LANES, LANES), 0)
    c2 = lax.broadcasted_iota(jnp.int32, (LANES, LANES), 1)
    same_head = (r2 >> 6) == (c2 >> 6)
    eye = r2 == c2
    eye_pk = jnp.where(row == src, 1.0, 0.0)
    first_row = (lax.broadcasted_iota(jnp.int32, (tc, LANES), 0) & (L - 1)) == 0

    def stack2(t):
        return jnp.concatenate([jnp.where(head0, t, 0.0), jnp.where(head0, 0.0, t)], axis=0)

    def side2(x, y):
        return jnp.concatenate([stack2(x), stack2(y)], axis=1).astype(BF16)

    def phase_a(trip, carry):
        hps = [trip * SCAN_PAIRS + i for i in range(SCAN_PAIRS)]
        cut = lambda t: [t[c * L:(c + 1) * L] for c in range(n_chunks)]
        rt, kt, at, bt, vv, w_last = [], [], [], [], [], []
        for hp in hps:
            kn = kn_ref[0, hp]
            cs = cs_ref[0, hp]
            cs_prev = jnp.where(first_row, 0.0, pltpu.roll(cs, 1, axis=0))
            w_t = jnp.exp(cs)
            w_inv = jnp.exp(-cs)
            rt += cut(r_ref[0, hp] * w_t)
            kt += cut(k_ref[0, hp] * w_inv)
            at += cut(-kn * jnp.exp(cs_prev))
            bt += cut(kn * a_ref[0, hp] * w_inv)
            vv += cut(v_ref[0, hp])
            w_last += [w_t[(c + 1) * L - 1:(c + 1) * L] for c in range(n_chunks)]
        chunks = range(SCAN_PAIRS * n_chunks)

        sc = [_dot_nt(jnp.concatenate([at[c], rt[c]], axis=0).astype(BF16),
                      jnp.concatenate([stack2(bt[c]), stack2(kt[c])], axis=0).astype(BF16))
              for c in chunks]
        a_ab = [jnp.where(strict, s[:L, :LANES], 0.0) for s in sc]
        a_ak = [jnp.where(strict, s[:L, LANES:], 0.0).astype(BF16) for s in sc]
        a_rb = [jnp.where(incl, s[L:, :LANES], 0.0).astype(BF16) for s in sc]
        a_rk = [jnp.where(incl, s[L:, LANES:], 0.0).astype(BF16) for s in sc]

        npow = a_ab
        inv = [eye_pk + n for n in npow]
        nbd = [stack2(n).astype(BF16) for n in npow]
        npow = [_dot(n.astype(BF16), b) for n, b in zip(npow, nbd)]
        for _ in range(int(math.log2(L)) - 2):
            nbd = [stack2(n).astype(BF16) for n in npow]
            both = [_dot(jnp.concatenate([n, i], axis=0).astype(BF16), b)
                    for n, i, b in zip(npow, inv, nbd)]
            npow = [t[:L] for t in both]
            inv = [i + t[L:] for i, t in zip(inv, both)]
        nbd = [stack2(n).astype(BF16) for n in npow]
        t_pk = [(i + _dot(i.astype(BF16), b)).astype(BF16) for i, b in zip(inv, nbd)]

        v2 = [stack2(t).astype(BF16) for t in vv]
        av = [_dot(jnp.concatenate([a_ak[c], a_rk[c]], axis=0), v2[c]) for c in chunks]
        q0 = [t[:L] for t in av]
        arkv = [t[L:] for t in av]
        pq = [_dot(t_pk[c], side2(at[c], q0[c])) for c in chunks]
        ar = [_dot(a_rb[c], side2(pq[c][:, :LANES], pq[c][:, LANES:])) for c in chunks]
        mc = [_dot(jnp.concatenate([bt[c] * w_last[c], kt[c] * w_last[c]], axis=0).T.astype(BF16),
                   jnp.concatenate(
                       [pq[c], jnp.concatenate([jnp.zeros_like(vv[c]), vv[c]], axis=1)], axis=0).astype(BF16))
              for c in chunks]
        for i in chunks:
            hp, c = hps[i // n_chunks], i % n_chunks
            rows = pl.ds(c * L, L)
            rhat_ref[hp, rows, :] = (rt[i] + ar[i][:, :LANES]).astype(BF16)
            y_ref[hp, rows, :] = ar[i][:, LANES:] + arkv[i]
            m_ref[hp, c] = (jnp.where(same_head, mc[i][:, :LANES], 0.0)
                            + jnp.where(eye, w_last[i], 0.0)).astype(BF16)
            c_ref[hp, c] = jnp.where(same_head, mc[i][:, LANES:], 0.0)
        return carry

    lax.fori_loop(0, HEAD_PAIRS // SCAN_PAIRS, phase_a, 0)

    def phase_b(c, carry):
        rows = pl.ds(pl.multiple_of(c * L, L), L)
        pairs = range(HEAD_PAIRS)
        sb = [s_ref[hp].astype(BF16) for hp in pairs]
        y = [_dot(rhat_ref[hp, rows, :], sb[hp]) for hp in pairs]
        s_new = [_dot(m_ref[hp, c], sb[hp]) for hp in pairs]
        for hp in pairs:
            y_ref[hp, rows, :] = y_ref[hp, rows, :] + y[hp]
            s_ref[hp] = s_new[hp] + c_ref[hp, c]
        return carry

    lax.fori_loop(0, n_chunks, phase_b, 0)

    gmean2 = gmean_ref[...]

    def head_mean(t):
        hi = t.astype(BF16)
        lo = (t - hi.astype(F32)).astype(BF16)
        return _dot(jnp.concatenate([hi, lo], axis=1), gmean2)

    pairs = range(HEAD_PAIRS)
    ys = [y_ref[hp] for hp in pairs]
    ds = [y - head_mean(y) for y in ys]
    var = [head_mean(d * d) for d in ds]
    for hp in pairs:
        y = ds[hp] * lax.rsqrt(var[hp] + GN_EPS) * lng_ref[hp] + lnb_ref[hp]
        out_ref[0, hp] = ((y + bonus_ref[0, hp]) * g_ref[0, hp]).astype(BF16)


def _rwkv_scan(r, k, v, kn, a, cs, bonus, g, lnx_g, lnx_b, tc):
    B, HP, T, _ = r.shape
    gmean = _group_matrix(LANES, HEAD_DIM, 1.0 / HEAD_DIM)
    gmean = jnp.concatenate([gmean, gmean], axis=0)
    n_chunks = tc // CHUNK
    return pl.pallas_call(
        _rwkv_scan_kernel,
        grid=(B, T // tc),
        in_specs=[_slab_spec(tc)] * 8 + [_const_spec((HP, 1, LANES))] * 2 + [_const_spec((2 * LANES, LANES))],
        out_specs=_slab_spec(tc),
        out_shape=jax.ShapeDtypeStruct((B, HP, T, LANES), BF16),
        scratch_shapes=[pltpu.VMEM((HP, LANES, LANES), F32),
                        pltpu.VMEM((HP, tc, LANES), BF16),
                        pltpu.VMEM((HP, tc, LANES), F32),
                        pltpu.VMEM((HP, n_chunks, LANES, LANES), BF16),
                        pltpu.VMEM((HP, n_chunks, LANES, LANES), F32)],
        compiler_params=_params("arbitrary", "arbitrary"),
        name="rwkv_scan",
    )(r, k, v, kn, a, cs, bonus, g, lnx_g.reshape(HP, 1, LANES), lnx_b.reshape(HP, 1, LANES), gmean)


def _diff_in_kernel(x_ref, ng_ref, w_ref, qg_ref, kg_ref, gmean_ref, q_out, k_out, v_out, qm_out):
    C = MIX_WIDTH
    h = _rms_rows(x_ref[0], ng_ref[...]).astype(BF16)
    proj = _dot(h, w_ref[...])
    gmean = gmean_ref[...]

    def head_norm(t):
        ms = jnp.concatenate(
            [_dot(jnp.square(t[:, i:i + GROUP_COLS]).astype(BF16), gmean)
             for i in range(0, C, GROUP_COLS)], axis=1)
        return t * lax.rsqrt(ms + RMS_EPS)

    q_out[0] = (head_norm(proj[:, :C]) * qg_ref[...]).astype(BF16)
    k_out[0] = (head_norm(proj[:, C:2 * C]) * kg_ref[...]).astype(BF16)
    v_out[0] = proj[:, 2 * C:3 * C].T.astype(BF16)
    qm_out[0] = proj[:, 3 * C:]


def _diff_in(x, norm_g, w_in, q_g, k_g, tm):
    B, T, D = x.shape
    C = MIX_WIDTH
    n_in = w_in.shape[1]
    gmean = _group_matrix(GROUP_COLS, HEAD_DIM, 1.0 / HEAD_DIM)
    qg = jnp.tile(q_g.reshape(-1), HEAD_PAIRS).reshape(1, C) * (HEAD_DIM ** -0.5 * LOG2_E)
    kg = jnp.tile(k_g.reshape(-1), HEAD_PAIRS).reshape(1, C)
    tile = pl.BlockSpec((1, tm, C), lambda b, t: (b, t, 0))
    big = jax.ShapeDtypeStruct((B, T, C), BF16)
    return pl.pallas_call(
        _diff_in_kernel,
        grid=(B, T // tm),
        in_specs=[pl.BlockSpec((1, tm, D), lambda b, t: (b, t, 0)),
                  _const_spec((1, D)), _const_spec((D, n_in)),
                  _const_spec((1, C)), _const_spec((1, C)), _const_spec((GROUP_COLS, GROUP_COLS))],
        out_specs=[tile, tile, pl.BlockSpec((1, C, tm), lambda b, t: (b, 0, t)),
                   pl.BlockSpec((1, tm, MEM_WIDTH), lambda b, t: (b, t, 0))],
        out_shape=[big, big, jax.ShapeDtypeStruct((B, C, T), BF16),
                   jax.ShapeDtypeStruct((B, T, MEM_WIDTH), F32)],
        compiler_params=_params("arbitrary", "arbitrary"),
        name="diff_in",
    )(x, norm_g.reshape(1, D), w_in.astype(BF16), qg, kg, gmean)


def _diff_flash_kernel(q_ref, k_ref, vt_ref, qfeat_ref, kfeat_ref,
                       lq1_ref, lk1_ref, lq2_ref, lk2_ref, sg_ref,
                       out_ref, sa_ref, sb_ref, *, lambda_init, tq, tk):
    seq = q_ref.shape[1]
    lam = (jnp.exp(jnp.sum(lq1_ref[...] * lk1_ref[...], axis=-1, keepdims=True))
           - jnp.exp(jnp.sum(lq2_ref[...] * lk2_ref[...], axis=-1, keepdims=True)) + lambda_init)
    th = tk // 2
    dim = lax.broadcasted_iota(jnp.int32, (LANES, tq), 0)
    key_loc = lax.broadcasted_iota(jnp.int32, (th, 1), 0)
    q_loc = lax.broadcasted_iota(jnp.int32, (1, 2 * tq), 1) & (tq - 1)
    qfeat = jnp.concatenate([qfeat_ref[0]] * (2 * tq // LANES), axis=1)
    ones_rows = jnp.ones((ONES_ROWS, th), BF16)

    def stacked_queries(q_start):
        qt = q_ref[0, pl.ds(pl.multiple_of(q_start, tq), tq), :].astype(F32).T
        zero = jnp.zeros_like(qt)
        top = jnp.concatenate(
            [jnp.where(dim < HEAD_DIM, qt, zero), jnp.where(dim < HEAD_DIM, zero, qt)], axis=1)
        return jnp.concatenate([top.astype(BF16), qfeat], axis=0)

    def tile_scores(qs, k_start):
        rows = pl.ds(pl.multiple_of(k_start, th), th)
        return _dot(jnp.concatenate([k_ref[0, rows, :], kfeat_ref[rows, :]], axis=1), qs)

    sa_ref[...] = tile_scores(stacked_queries(0), 0)

    def q_tile(qi, tile_carry):
        q_start = pl.multiple_of(qi * tq, tq)
        qs = stacked_queries(q_start)
        scores = functools.partial(tile_scores, qs)

        def softmax_pv(s, k_start, carry, masked):
            m, acc = carry
            if masked:
                s = jnp.where(key_loc + (k_start - q_start) <= q_loc, s, NEG_BIG)
            m_new = jnp.maximum(m, jnp.max(s, axis=0, keepdims=True))
            p = jnp.exp2(s - m_new).astype(BF16)
            vt = vt_ref[0, :, pl.ds(pl.multiple_of(k_start, th), th)]
            pv = _dot(jnp.concatenate([vt, ones_rows], axis=0), p)
            return m_new, jnp.exp2(m - m_new) * acc + pv

        def full_block(j, carry):
            k0 = j * tk
            sb_ref[...] = scores(k0 + th)
            carry = softmax_pv(sa_ref[...], k0, carry, False)
            sa_ref[...] = scores(k0 + tk)
            return softmax_pv(sb_ref[...], k0 + th, carry, False)

        n_full = q_start // tk
        carry = (jnp.full((1, 2 * tq), NEG_BIG, F32), jnp.zeros((LANES + ONES_ROWS, 2 * tq), F32))
        carry = lax.fori_loop(0, n_full, full_block, carry)
        k0 = n_full * tk
        second_half = q_start + tq > k0 + th

        @pl.when(second_half)
        def _():
            sb_ref[...] = scores(k0 + th)

        carry = softmax_pv(sa_ref[...], k0, carry, True)
        sa_ref[...] = tile_scores(stacked_queries(jnp.minimum(q_start + tq, seq - tq)), 0)
        _, acc = lax.cond(second_half,
                          lambda c: softmax_pv(sb_ref[...], k0 + th, c, True),
                          lambda c: c, carry)

        o = acc[:LANES] / acc[LANES:LANES + 1]
        o = o[:, :tq] - lam * o[:, tq:]
        o = o * lax.rsqrt(jnp.mean(o * o, axis=0, keepdims=True) + RMS_EPS) * sg_ref[...]
        out_ref[0, 0, pl.ds(q_start, tq), :] = (o * (1.0 - lambda_init)).T.astype(BF16)
        return tile_carry

    lax.fori_loop(0, seq // tq, q_tile, 0)


def _diff_flash(q, k, vt, lq1, lk1, lq2, lk2, subln_g, lambda_init, tq, tk):
    B, T, C = q.shape
    H = C // LANES
    slopes = jnp.exp2(-8.0 * jnp.arange(1, H + 1, dtype=F32) / H) * LOG2_E
    assert T // FEAT_RADIX <= 256
    s1 = slopes.astype(BF16)
    r1 = slopes - s1.astype(F32)
    s2 = r1.astype(BF16)
    s3 = (r1 - s2.astype(F32)).astype(BF16)
    pieces = jnp.stack([s1, s2, s3], axis=1).astype(F32)
    qfeat = jnp.concatenate([pieces * FEAT_RADIX, pieces, jnp.zeros((H, LANES - 6), F32)], axis=1)
    qfeat = jnp.broadcast_to(qfeat.astype(BF16)[:, :, None], (H, LANES, LANES))
    pos = jnp.arange(T)
    hi = jnp.broadcast_to((pos // FEAT_RADIX)[:, None], (T, 3))
    lo = jnp.broadcast_to((pos % FEAT_RADIX)[:, None], (T, 3))
    kfeat = jnp.concatenate([hi, lo, jnp.zeros((T, LANES - 6), jnp.int32)], axis=1).astype(BF16)
    lvec = _const_spec((1, HEAD_DIM))
    seq = pl.BlockSpec((1, T, LANES), lambda b, h: (b, 0, h))
    return pl.pallas_call(
        functools.partial(_diff_flash_kernel, lambda_init=lambda_init, tq=tq, tk=tk),
        grid=(B, H),
        in_specs=[seq, seq, pl.BlockSpec((1, LANES, T), lambda b, h: (b, h, 0)),
                  pl.BlockSpec((1, LANES, LANES), lambda b, h: (h, 0, 0)), _const_spec((T, LANES)),
                  lvec, lvec, lvec, lvec, _const_spec((LANES, 1))],
        out_specs=pl.BlockSpec((1, 1, T, LANES), lambda b, h: (b, h, 0, 0)),
        out_shape=jax.ShapeDtypeStruct((B, H, T, LANES), BF16),
        scratch_shapes=[pltpu.VMEM((tk // 2, 2 * tq), F32)] * 2,
        compiler_params=_params("arbitrary", "arbitrary"),
        name="diff_flash",
    )(q, k, vt, qfeat, kfeat,
      lq1.reshape(1, -1), lk1.reshape(1, -1), lq2.reshape(1, -1), lk2.reshape(1, -1),
      subln_g.reshape(LANES, 1))


def _attn_out_kernel(mix_ref, qm_ref, km_ref, vm_ref, x_ref, qg_ref, kg_ref, wa_ref, wb_ref, gmean_ref,
                     out_ref):
    qm = qm_ref[0]
    ms = _dot((qm * qm).astype(BF16), gmean_ref[...])
    qn = qm * lax.rsqrt(ms + RMS_EPS) * qg_ref[...]
    kb = (km_ref[0] * kg_ref[...]).astype(BF16)
    vb = vm_ref[0].astype(BF16)
    lane = lax.broadcasted_iota(jnp.int32, qn.shape, 1)
    mem = jnp.zeros_like(qn)
    for hd in range(MEM_HEADS):
        in_head = (lane >= hd * HEAD_DIM) & (lane < (hd + 1) * HEAD_DIM)
        s = _dot_nt(jnp.where(in_head, qn, 0.0).astype(BF16), kb)
        p = jnp.exp(s - jnp.max(s, axis=-1, keepdims=True))
        o = _dot(p.astype(BF16), vb) / jnp.sum(p, axis=-1, keepdims=True)
        mem = jnp.where(in_head, o, mem)
    mix = jnp.concatenate([mix_ref[0, hp] for hp in range(HEAD_PAIRS)], axis=1)
    y = _dot(mix, wa_ref[...]) + _dot(mem.astype(BF16), wb_ref[...])
    out_ref[0] = x_ref[0] + y


def _attn_out(mix, qm, k_mem, v_mem, x, q_g, k_g, w_out, tm):
    B, T, D = x.shape
    M = k_mem.shape[1]
    gmean = _group_matrix(MEM_WIDTH, HEAD_DIM, 1.0 / HEAD_DIM)
    qg = jnp.tile(q_g, MEM_HEADS).reshape(1, MEM_WIDTH) * HEAD_DIM ** -0.5
    kg = jnp.tile(k_g, MEM_HEADS).reshape(1, MEM_WIDTH)
    w = w_out.astype(BF16)
    rows = lambda n: pl.BlockSpec((1, tm, n), lambda b, t: (b, t, 0))
    memspec = pl.BlockSpec((1, M, MEM_WIDTH), lambda b, t: (b, 0, 0))
    return pl.pallas_call(
        _attn_out_kernel,
        grid=(B, T // tm),
        in_specs=[_slab_spec(tm), rows(MEM_WIDTH), memspec, memspec, rows(D),
                  _const_spec((1, MEM_WIDTH)), _const_spec((1, MEM_WIDTH)),
                  _const_spec((MIX_WIDTH, D)), _const_spec((MEM_WIDTH, D)),
                  _const_spec((MEM_WIDTH, MEM_WIDTH))],
        out_specs=rows(D),
        out_shape=jax.ShapeDtypeStruct((B, T, D), F32),
        compiler_params=_params("arbitrary", "arbitrary"),
        name="attn_out",
    )(mix, qm, k_mem, v_mem, x, qg, kg, w[:MIX_WIDTH], w[MIX_WIDTH:], gmean)


def _ffn_kernel(x_ref, ng_ref, w1_ref, w2_ref, out_ref, *, ff_chunk):
    x = x_ref[0]
    h = _rms_rows(x, ng_ref[...]).astype(BF16)
    acc = x
    for c in range(0, w1_ref.shape[1], ff_chunk):
        u = jnp.maximum(_dot(h, w1_ref[:, c:c + ff_chunk]), 0.0)
        acc = acc + _dot((u * u).astype(BF16), w2_ref[c:c + ff_chunk, :])
    out_ref[0] = acc


def _ffn(x, norm_g, w1, w2, tm):
    B, T, D = x.shape
    F = w1.shape[1]
    rows = pl.BlockSpec((1, tm, D), lambda b, t: (b, t, 0))
    return pl.pallas_call(
        functools.partial(_ffn_kernel, ff_chunk=D),
        grid=(B, T // tm),
        in_specs=[rows, _const_spec((1, D)), _const_spec((D, F)), _const_spec((F, D))],
        out_specs=rows,
        out_shape=jax.ShapeDtypeStruct((B, T, D), F32),
        compiler_params=_params("arbitrary", "arbitrary"),
        name="ffn",
    )(x, norm_g.reshape(1, D), w1.astype(BF16), w2.astype(BF16))


def _diff_lambda_init(layer):
    return 0.8 - 0.6 * math.exp(-0.3 * layer)


def kernel(x, mem, norm_mix_g, norm_ffn_g, w_out, w_ff1, w_ff2, mem_norm_g, w_mem_kv, mem_q_norm_g, mem_k_norm_g, rw_in, rw_mu, rw_w0, rw_w2, rw_a0, rw_a2, rw_g2, rw_k_k, rw_k_a, rw_r_k, rw_lnx_g, rw_lnx_b, df_in, df_q_norm_g, df_k_norm_g, df_lq1, df_lk1, df_lq2, df_lk2, df_subln_g):
    T = x.shape[1]
    tm = min(T, 512)
    depth = norm_mix_g.shape[0]
    k_mem, v_mem = _mem_kv(mem, mem_norm_g, w_mem_kv)
    for layer in range(depth):
        j = layer // 2
        if layer % 2 == 0:
            r, k, v, kn, a, cs, bonus, g, qm = _rwkv_in(
                x, norm_mix_g[layer], rw_in[j], rw_mu[j], rw_w0[j], rw_w2[j], rw_a0[j], rw_a2[j],
                rw_g2[j], rw_k_k[j], rw_k_a[j], rw_r_k[j], tm)
            mix = _rwkv_scan(r, k, v, kn, a, cs, bonus, g, rw_lnx_g[j], rw_lnx_b[j], min(T, 512))
        else:
            q, k, v, qm = _diff_in(x, norm_mix_g[layer], df_in[j], df_q_norm_g[j], df_k_norm_g[j],
                                   min(T, 1024))
            mix = _diff_flash(q, k, v, df_lq1[j], df_lk1[j], df_lq2[j], df_lk2[j], df_subln_g[j],
                              _diff_lambda_init(layer), min(T, 512), min(T, 1024))
        x = _attn_out(mix, qm, k_mem, v_mem, x, mem_q_norm_g[layer], mem_k_norm_g[layer], w_out[layer],
                      min(T, 1024))
        x = _ffn(x, norm_ffn_g[layer], w_ff1[layer], w_ff2[layer], min(T, 1024))
    return x
```

```python
import functools
import math

import jax
import jax.numpy as jnp
from jax import lax
from jax.experimental import pallas as pl
from jax.experimental.pallas import tpu as pltpu

F32 = jnp.float32
BF16 = jnp.bfloat16

D_MODEL = 1024
HEAD_DIM = 64
MIX_WIDTH = 768
MEM_WIDTH = 256
MEM_HEADS = 4
DECAY_LORA = 64
ICLR_LORA = 64
GATE_LORA = 128
RWKV_COLS = 3 * MIX_WIDTH + DECAY_LORA + ICLR_LORA + GATE_LORA
D_FF = 4 * D_MODEL
RMS_EPS = 1e-6
GN_EPS = 64e-5

LANES = 128
HEAD_PAIRS = MIX_WIDTH // LANES
GROUP_COLS = 256
CHUNK = 64
SCAN_PAIRS = 3
NEG_BIG = -1e30
LOG2_E = math.log2(math.e)
FEAT_RADIX = 32
ONES_ROWS = 16
VMEM_LIMIT = 56 * 1024 * 1024


def _dot(a, b):
    return jnp.dot(a, b, preferred_element_type=F32)


def _dot_nt(a, b):
    return lax.dot_general(a, b, (((1,), (1,)), ((), ())), preferred_element_type=F32)


def _dot_split(x, m):
    hi = x.astype(BF16)
    lo = (x - hi.astype(F32)).astype(BF16)
    return _dot(hi, m) + _dot(lo, m)


def _dot_split_left(m, x):
    hi = x.astype(BF16)
    lo = (x - hi.astype(F32)).astype(BF16)
    return _dot(m, hi) + _dot(m, lo)


def _sigmoid(x):
    return 1.0 / (1.0 + jnp.exp(-x))


def _rms_rows(x, g):
    return x * lax.rsqrt(jnp.mean(x * x, axis=-1, keepdims=True) + RMS_EPS) * g


def _group_matrix(n, group, value):
    i = jnp.arange(n) // group
    return jnp.where(i[:, None] == i[None, :], value, 0.0).astype(BF16)


def _params(*sem):
    return pltpu.CompilerParams(dimension_semantics=sem, vmem_limit_bytes=VMEM_LIMIT)


def _const_spec(shape):
    nd = len(shape)
    return pl.BlockSpec(shape, lambda *_: (0,) * nd)


def _slab_spec(tm):
    return pl.BlockSpec((1, HEAD_PAIRS, tm, LANES), lambda b, t: (b, 0, t, 0))


def _mem_kv_kernel(mem_ref, g_ref, w_ref, gm_ref, k_out, v_out):
    h = _rms_rows(mem_ref[0], g_ref[...])
    kv = _dot(h.astype(BF16), w_ref[...])
    k = kv[:, :MEM_WIDTH]
    ms = _dot_split(k * k, gm_ref[...])
    k_out[0] = k * lax.rsqrt(ms + RMS_EPS)
    v_out[0] = kv[:, MEM_WIDTH:]


def _mem_kv(mem, mem_norm_g, w_mem_kv):
    B, M, D = mem.shape
    gm = _group_matrix(MEM_WIDTH, HEAD_DIM, 1.0 / HEAD_DIM)
    out = jax.ShapeDtypeStruct((B, M, MEM_WIDTH), F32)
    return pl.pallas_call(
        _mem_kv_kernel,
        grid=(B,),
        in_specs=[pl.BlockSpec((1, M, D), lambda b: (b, 0, 0)),
                  _const_spec((1, D)), _const_spec((D, 2 * MEM_WIDTH)),
                  _const_spec((MEM_WIDTH, MEM_WIDTH))],
        out_specs=[pl.BlockSpec((1, M, MEM_WIDTH), lambda b: (b, 0, 0))] * 2,
        out_shape=[out, out],
        compiler_params=_params("arbitrary"),
        name="mem_kv",
    )(mem, mem_norm_g.reshape(1, D), w_mem_kv.astype(BF16), gm)


def _rwkv_in_kernel(x_ref, ng_ref, w_ref, mu_ref, w0_ref, w2_ref, a0_ref, a2_ref, g2_ref,
                    kk_ref, ka_ref, rk_ref, gsum_ref, tri_ref,
                    r_out, k_out, v_out, kn_out, a_out, cs_out, bonus_out, g_out, qm_out,
                    carry_ref):
    C = MIX_WIDTH
    tm = x_ref.shape[1]

    @pl.when(pl.program_id(1) == 0)
    def _():
        carry_ref[...] = jnp.zeros_like(carry_ref)

    h = _rms_rows(x_ref[0], ng_ref[...]).astype(BF16)
    proj = _dot(h, w_ref[...])
    cur = proj[:, :RWKV_COLS]
    row = lax.broadcasted_iota(jnp.int32, (tm, 1), 0)
    prev = jnp.where(row == 0, carry_ref[0:1, :], pltpu.roll(cur, 1, axis=0))
    carry_ref[0:1, :] = cur[tm - 1:tm, :]
    slab = cur + (prev - cur) * mu_ref[...]

    r = slab[:, 0:C]
    k = slab[:, C:2 * C]
    v = slab[:, 2 * C:3 * C]
    wa = slab[:, 3 * C:3 * C + DECAY_LORA + ICLR_LORA]
    gd = slab[:, 3 * C + DECAY_LORA + ICLR_LORA:RWKV_COLS]

    d = w0_ref[...] + _dot(jnp.tanh(wa).astype(BF16), w2_ref[...])
    log_decay = -math.exp(-0.5) * _sigmoid(d)
    a = _sigmoid(a0_ref[...] + _dot(wa.astype(BF16), a2_ref[...]))
    g = _dot(_sigmoid(gd).astype(BF16), g2_ref[...])

    gsum = gsum_ref[...]

    def head_sum(t):
        return jnp.concatenate(
            [_dot(t[:, i:i + GROUP_COLS].astype(BF16), gsum) for i in range(0, C, GROUP_COLS)], axis=1)

    kn = k * kk_ref[...]
    kn = kn * lax.rsqrt(jnp.maximum(head_sum(kn * kn), 1e-24))
    k2 = k * (1.0 + (a - 1.0) * ka_ref[...])
    bonus = head_sum(r * k2 * rk_ref[...]) * v

    cs = jnp.concatenate([_dot_split_left(tri_ref[...], log_decay[i:i + GROUP_COLS])
                          for i in range(0, tm, GROUP_COLS)], axis=0)
    outs = ((r_out, r), (k_out, k2), (v_out, v), (kn_out, kn), (a_out, a), (cs_out, cs),
            (bonus_out, bonus), (g_out, g))
    for ref, val in outs:
        for hp in range(HEAD_PAIRS):
            ref[0, hp] = val[:, hp * LANES:(hp + 1) * LANES]
    qm_out[0] = proj[:, RWKV_COLS:]


def _rwkv_in(x, norm_g, w_in, mu, w0, w2, a0, a2, g2, k_k, k_a, r_k, tm):
    B, T, D = x.shape
    C = MIX_WIDTH
    n_in = w_in.shape[1]
    zeros = jnp.zeros((DECAY_LORA, C), F32)
    w2p = jnp.concatenate([w2, zeros], axis=0).astype(BF16)
    a2p = jnp.concatenate([zeros, a2], axis=0).astype(BF16)
    gsum = _group_matrix(GROUP_COLS, HEAD_DIM, 1.0)
    t_idx = jnp.arange(GROUP_COLS)
    tri = ((t_idx[:, None] // CHUNK == t_idx[None, :] // CHUNK)
           & (t_idx[None, :] <= t_idx[:, None])).astype(BF16)
    row = lambda p: p.reshape(1, -1)
    big = jax.ShapeDtypeStruct((B, HEAD_PAIRS, T, LANES), F32)
    return pl.pallas_call(
        _rwkv_in_kernel,
        grid=(B, T // tm),
        in_specs=[pl.BlockSpec((1, tm, D), lambda b, t: (b, t, 0)),
                  _const_spec((1, D)), _const_spec((D, n_in)), _const_spec((1, RWKV_COLS)),
                  _const_spec((1, C)), _const_spec((2 * DECAY_LORA, C)),
                  _const_spec((1, C)), _const_spec((2 * ICLR_LORA, C)),
                  _const_spec((GATE_LORA, C)),
                  _const_spec((1, C)), _const_spec((1, C)), _const_spec((1, C)),
                  _const_spec((GROUP_COLS, GROUP_COLS)), _const_spec((GROUP_COLS, GROUP_COLS))],
        out_specs=[_slab_spec(tm)] * 8 + [pl.BlockSpec((1, tm, MEM_WIDTH), lambda b, t: (b, t, 0))],
        out_shape=[big] * 8 + [jax.ShapeDtypeStruct((B, T, MEM_WIDTH), F32)],
        scratch_shapes=[pltpu.VMEM((8, RWKV_COLS), F32)],
        compiler_params=_params("arbitrary", "arbitrary"),
        name="rwkv_in",
    )(x, row(norm_g), w_in.astype(BF16), row(mu), row(w0), w2p, row(a0), a2p, g2.astype(BF16),
      row(k_k), row(k_a), row(r_k), gsum, tri)


def _rwkv_scan_kernel(r_ref, k_ref, v_ref, kn_ref, a_ref, cs_ref, bonus_ref, g_ref,
                      lng_ref, lnb_ref, gmean_ref, out_ref,
                      s_ref, rhat_ref, y_ref, m_ref, c_ref):
    L = CHUNK
    tc = r_ref.shape[2]
    n_chunks = tc // L

    @pl.when(pl.program_id(1) == 0)
    def _():
        s_ref[...] = jnp.zeros_like(s_ref)

    lane = lax.broadcasted_iota(jnp.int32, (L, LANES), 1)
    row = lax.broadcasted_iota(jnp.int32, (L, LANES), 0)
    head0 = lane < HEAD_DIM
    src = lane & (HEAD_DIM - 1)
    strict = row > src
    incl = row >= src
    r2 = lax.broadcasted_iota(jnp.int32, (LANES, LANES), 0)
    c2 = lax.broadcasted_iota(jnp.int32, (LANES, LANES), 1)
    same_head = (r2 >> 6) == (c2 >> 6)
    eye = r2 == c2
    eye_pk = jnp.where(row == src, 1.0, 0.0)
    first_row = (lax.broadcasted_iota(jnp.int32, (tc, LANES), 0) & (L - 1)) == 0

    def stack2(t):
        return jnp.concatenate([jnp.where(head0, t, 0.0), jnp.where(head0, 0.0, t)], axis=0)

    def side2(x, y):
        return jnp.concatenate([stack2(x), stack2(y)], axis=1).astype(BF16)

    def phase_a(trip, carry):
        hps = [trip * SCAN_PAIRS + i for i in range(SCAN_PAIRS)]
        cut = lambda t: [t[c * L:(c + 1) * L] for c in range(n_chunks)]
        rt, kt, at, bt, vv, w_last = [], [], [], [], [], []
        for hp in hps:
            kn = kn_ref[0, hp]
            cs = cs_ref[0, hp]
            cs_prev = jnp.where(first_row, 0.0, pltpu.roll(cs, 1, axis=0))
            w_t = jnp.exp(cs)
            w_inv = jnp.exp(-cs)
            rt += cut(r_ref[0, hp] * w_t)
            kt += cut(k_ref[0, hp] * w_inv)
            at += cut(-kn * jnp.exp(cs_prev))
            bt += cut(kn * a_ref[0, hp] * w_inv)
            vv += cut(v_ref[0, hp])
            w_last += [w_t[(c + 1) * L - 1:(c + 1) * L] for c in range(n_chunks)]
        chunks = range(SCAN_PAIRS * n_chunks)

        sc = [_dot_nt(jnp.concatenate([at[c], rt[c]], axis=0).astype(BF16),
                      jnp.concatenate([stack2(bt[c]), stack2(kt[c])], axis=0).astype(BF16))
              for c in chunks]
        a_ab = [jnp.where(strict, s[:L, :LANES], 0.0) for s in sc]
        a_ak = [jnp.where(strict, s[:L, LANES:], 0.0).astype(BF16) for s in sc]
        a_rb = [jnp.where(incl, s[L:, :LANES], 0.0).astype(BF16) for s in sc]
        a_rk = [jnp.where(incl, s[L:, LANES:], 0.0).astype(BF16) for s in sc]

        npow = a_ab
        inv = [eye_pk + n for n in npow]
        nbd = [stack2(n).astype(BF16) for n in npow]
        npow = [_dot(n.astype(BF16), b) for n, b in zip(npow, nbd)]
        for _ in range(int(math.log2(L)) - 2):
            nbd = [stack2(n).astype(BF16) for n in npow]
            both = [_dot(jnp.concatenate([n, i], axis=0).astype(BF16), b)
                    for n, i, b in zip(npow, inv, nbd)]
            npow = [t[:L] for t in both]
            inv = [i + t[L:] for i, t in zip(inv, both)]
        nbd = [stack2(n).astype(BF16) for n in npow]
        t_pk = [(i + _dot(i.astype(BF16), b)).astype(BF16) for i, b in zip(inv, nbd)]

        v2 = [stack2(t).astype(BF16) for t in vv]
        av = [_dot(jnp.concatenate([a_ak[c], a_rk[c]], axis=0), v2[c]) for c in chunks]
        q0 = [t[:L] for t in av]
        arkv = [t[L:] for t in av]
        pq = [_dot(t_pk[c], side2(at[c], q0[c])) for c in chunks]
        ar = [_dot(a_rb[c], side2(pq[c][:, :LANES], pq[c][:, LANES:])) for c in chunks]
        mc = [_dot(jnp.concatenate([bt[c] * w_last[c], kt[c] * w_last[c]], axis=0).T.astype(BF16),
                   jnp.concatenate(
                       [pq[c], jnp.concatenate([jnp.zeros_like(vv[c]), vv[c]], axis=1)], axis=0).astype(BF16))
              for c in chunks]
        for i in chunks:
            hp, c = hps[i // n_chunks], i % n_chunks
            rows = pl.ds(c * L, L)
            rhat_ref[hp, rows, :] = (rt[i] + ar[i][:, :LANES]).astype(BF16)
            y_ref[hp, rows, :] = ar[i][:, LANES:] + arkv[i]
            m_ref[hp, c] = (jnp.where(same_head, mc[i][:, :LANES], 0.0)
                            + jnp.where(eye, w_last[i], 0.0)).astype(BF16)
            c_ref[hp, c] = jnp.where(same_head, mc[i][:, LANES:], 0.0)
        return carry

    lax.fori_loop(0, HEAD_PAIRS // SCAN_PAIRS, phase_a, 0)

    def phase_b(c, carry):
        rows = pl.ds(pl.multiple_of(c * L, L), L)
        pairs = range(HEAD_PAIRS)
        sb = [s_ref[hp].astype(BF16) for hp in pairs]
        y = [_dot(rhat_ref[hp, rows, :], sb[hp]) for hp in pairs]
        s_new = [_dot(m_ref[hp, c], sb[hp]) for hp in pairs]
        for hp in pairs:
            y_ref[hp, rows, :] = y_ref[hp, rows, :] + y[hp]
            s_ref[hp] = s_new[hp] + c_ref[hp, c]
        return carry

    lax.fori_loop(0, n_chunks, phase_b, 0)

    gmean2 = gmean_ref[...]

    def head_mean(t):
        hi = t.astype(BF16)
        lo = (t - hi.astype(F32)).astype(BF16)
        return _dot(jnp.concatenate([hi, lo], axis=1), gmean2)

    pairs = range(HEAD_PAIRS)
    ys = [y_ref[hp] for hp in pairs]
    ds = [y - head_mean(y) for y in ys]
    var = [head_mean(d * d) for d in ds]
    for hp in pairs:
        y = ds[hp] * lax.rsqrt(var[hp] + GN_EPS) * lng_ref[hp] + lnb_ref[hp]
        out_ref[0, hp] = ((y + bonus_ref[0, hp]) * g_ref[0, hp]).astype(BF16)


def _rwkv_scan(r, k, v, kn, a, cs, bonus, g, lnx_g, lnx_b, tc):
    B, HP, T, _ = r.shape
    gmean = _group_matrix(LANES, HEAD_DIM, 1.0 / HEAD_DIM)
    gmean = jnp.concatenate([gmean, gmean], axis=0)
    n_chunks = tc // CHUNK
    return pl.pallas_call(
        _rwkv_scan_kernel,
        grid=(B, T // tc),
        in_specs=[_slab_spec(tc)] * 8 + [_const_spec((HP, 1, LANES))] * 2 + [_const_spec((2 * LANES, LANES))],
        out_specs=_slab_spec(tc),
        out_shape=jax.ShapeDtypeStruct((B, HP, T, LANES), BF16),
        scratch_shapes=[pltpu.VMEM((HP, LANES, LANES), F32),
                        pltpu.VMEM((HP, tc, LANES), BF16),
                        pltpu.VMEM((HP, tc, LANES), F32),
                        pltpu.VMEM((HP, n_chunks, LANES, LANES), BF16),
                        pltpu.VMEM((HP, n_chunks, LANES, LANES), F32)],
        compiler_params=_params("arbitrary", "arbitrary"),
        name="rwkv_scan",
    )(r, k, v, kn, a, cs, bonus, g, lnx_g.reshape(HP, 1, LANES), lnx_b.reshape(HP, 1, LANES), gmean)


def _diff_in_kernel(x_ref, ng_ref, w_ref, qg_ref, kg_ref, gmean_ref, q_out, k_out, v_out, qm_out):
    C = MIX_WIDTH
    h = _rms_rows(x_ref[0], ng_ref[...]).astype(BF16)
    proj = _dot(h, w_ref[...])
    gmean = gmean_ref[...]

    def head_norm(t):
        ms = jnp.concatenate(
            [_dot(jnp.square(t[:, i:i + GROUP_COLS]).astype(BF16), gmean)
             for i in range(0, C, GROUP_COLS)], axis=1)
        return t * lax.rsqrt(ms + RMS_EPS)

    q_out[0] = (head_norm(proj[:, :C]) * qg_ref[...]).astype(BF16)
    k_out[0] = (head_norm(proj[:, C:2 * C]) * kg_ref[...]).astype(BF16)
    v_out[0] = proj[:, 2 * C:3 * C].T.astype(BF16)
    qm_out[0] = proj[:, 3 * C:]


def _diff_in(x, norm_g, w_in, q_g, k_g, tm):
    B, T, D = x.shape
    C = MIX_WIDTH
    n_in = w_in.shape[1]
    gmean = _group_matrix(GROUP_COLS, HEAD_DIM, 1.0 / HEAD_DIM)
    qg = jnp.tile(q_g.reshape(-1), HEAD_PAIRS).reshape(1, C) * (HEAD_DIM ** -0.5 * LOG2_E)
    kg = jnp.tile(k_g.reshape(-1), HEAD_PAIRS).reshape(1, C)
    tile = pl.BlockSpec((1, tm, C), lambda b, t: (b, t, 0))
    big = jax.ShapeDtypeStruct((B, T, C), BF16)
    return pl.pallas_call(
        _diff_in_kernel,
        grid=(B, T // tm),
        in_specs=[pl.BlockSpec((1, tm, D), lambda b, t: (b, t, 0)),
                  _const_spec((1, D)), _const_spec((D, n_in)),
                  _const_spec((1, C)), _const_spec((1, C)), _const_spec((GROUP_COLS, GROUP_COLS))],
        out_specs=[tile, tile, pl.BlockSpec((1, C, tm), lambda b, t: (b, 0, t)),
                   pl.BlockSpec((1, tm, MEM_WIDTH), lambda b, t: (b, t, 0))],
        out_shape=[big, big, jax.ShapeDtypeStruct((B, C, T), BF16),
                   jax.ShapeDtypeStruct((B, T, MEM_WIDTH), F32)],
        compiler_params=_params("arbitrary", "arbitrary"),
        name="diff_in",
    )(x, norm_g.reshape(1, D), w_in.astype(BF16), qg, kg, gmean)


def _diff_flash_kernel(q_ref, k_ref, vt_ref, qfeat_ref, kfeat_ref,
                       lq1_ref, lk1_ref, lq2_ref, lk2_ref, sg_ref,
                       out_ref, sa_ref, sb_ref, *, lambda_init, tq):
    seq = q_ref.shape[1]
    lam = (jnp.exp(jnp.sum(lq1_ref[...] * lk1_ref[...], axis=-1, keepdims=True))
           - jnp.exp(jnp.sum(lq2_ref[...] * lk2_ref[...], axis=-1, keepdims=True)) + lambda_init)
    dim = lax.broadcasted_iota(jnp.int32, (LANES, tq), 0)
    key_loc = lax.broadcasted_iota(jnp.int32, (tq, 1), 0)
    q_loc = lax.broadcasted_iota(jnp.int32, (1, 2 * tq), 1) & (tq - 1)
    qfeat = jnp.concatenate([qfeat_ref[0]] * (2 * tq // LANES), axis=1)
    ones_rows = jnp.ones((ONES_ROWS, tq), BF16)

    def stacked_queries(t):
        qt = q_ref[0, t * tq:(t + 1) * tq, :].astype(F32).T
        zero = jnp.zeros_like(qt)
        top = jnp.concatenate(
            [jnp.where(dim < HEAD_DIM, qt, zero), jnp.where(dim < HEAD_DIM, zero, qt)], axis=1)
        return jnp.concatenate([top.astype(BF16), qfeat], axis=0)

    def scores(qs, h):
        rows = slice(h * tq, (h + 1) * tq)
        return _dot(jnp.concatenate([k_ref[0, rows, :], kfeat_ref[rows, :]], axis=1), qs)

    def softmax_pv(s, h, carry, masked):
        m, acc = carry
        if masked:
            s = jnp.where(key_loc <= q_loc, s, NEG_BIG)
        m_new = jnp.maximum(m, jnp.max(s, axis=0, keepdims=True))
        p = jnp.exp2(s - m_new).astype(BF16)
        vt = vt_ref[0, :, h * tq:(h + 1) * tq]
        pv = _dot(jnp.concatenate([vt, ones_rows], axis=0), p)
        return m_new, jnp.exp2(m - m_new) * acc + pv

    tasks = [(t, h) for t in range(seq // tq) for h in range(t + 1)]
    bufs = (sa_ref, sb_ref)
    fresh = (jnp.full((1, 2 * tq), NEG_BIG, F32), jnp.zeros((LANES + ONES_ROWS, 2 * tq), F32))
    qs = stacked_queries(0)
    bufs[0][...] = scores(qs, 0)
    carry = fresh
    for n, (t, h) in enumerate(tasks):
        if n + 1 < len(tasks):
            t_next, h_next = tasks[n + 1]
            qs_next = qs if t_next == t else stacked_queries(t_next)
            bufs[(n + 1) % 2][...] = scores(qs_next, h_next)
        carry = softmax_pv(bufs[n % 2][...], h, carry, h == t)
        if h == t:
            acc = carry[1]
            o = acc[:LANES] / acc[LANES:LANES + 1]
            o = o[:, :tq] - lam * o[:, tq:]
            o = o * lax.rsqrt(jnp.mean(o * o, axis=0, keepdims=True) + RMS_EPS) * sg_ref[...]
            out_ref[0, 0, t * tq:(t + 1) * tq, :] = (o * (1.0 - lambda_init)).T.astype(BF16)
            carry = fresh
        if n + 1 < len(tasks):
            qs = qs_next


def _diff_flash(q, k, vt, lq1, lk1, lq2, lk2, subln_g, lambda_init, tq):
    B, T, C = q.shape
    H = C // LANES
    slopes = jnp.exp2(-8.0 * jnp.arange(1, H + 1, dtype=F32) / H) * LOG2_E
    assert T // FEAT_RADIX <= 256
    s1 = slopes.astype(BF16)
    r1 = slopes - s1.astype(F32)
    s2 = r1.astype(BF16)
    s3 = (r1 - s2.astype(F32)).astype(BF16)
    pieces = jnp.stack([s1, s2, s3], axis=1).astype(F32)
    qfeat = jnp.concatenate([pieces * FEAT_RADIX, pieces, jnp.zeros((H, LANES - 6), F32)], axis=1)
    qfeat = jnp.broadcast_to(qfeat.astype(BF16)[:, :, None], (H, LANES, LANES))
    pos = jnp.arange(T)
    hi = jnp.broadcast_to((pos // FEAT_RADIX)[:, None], (T, 3))
    lo = jnp.broadcast_to((pos % FEAT_RADIX)[:, None], (T, 3))
    kfeat = jnp.concatenate([hi, lo, jnp.zeros((T, LANES - 6), jnp.int32)], axis=1).astype(BF16)
    lvec = _const_spec((1, HEAD_DIM))
    seq = pl.BlockSpec((1, T, LANES), lambda b, h: (b, 0, h))
    return pl.pallas_call(
        functools.partial(_diff_flash_kernel, lambda_init=lambda_init, tq=tq),
        grid=(B, H),
        in_specs=[seq, seq, pl.BlockSpec((1, LANES, T), lambda b, h: (b, h, 0)),
                  pl.BlockSpec((1, LANES, LANES), lambda b, h: (h, 0, 0)), _const_spec((T, LANES)),
                  lvec, lvec, lvec, lvec, _const_spec((LANES, 1))],
        out_specs=pl.BlockSpec((1, 1, T, LANES), lambda b, h: (b, h, 0, 0)),
        out_shape=jax.ShapeDtypeStruct((B, H, T, LANES), BF16),
        scratch_shapes=[pltpu.VMEM((tq, 2 * tq), F32)] * 2,
        compiler_params=_params("arbitrary", "arbitrary"),
        name="diff_flash",
    )(q, k, vt, qfeat, kfeat,
      lq1.reshape(1, -1), lk1.reshape(1, -1), lq2.reshape(1, -1), lk2.reshape(1, -1),
      subln_g.reshape(LANES, 1))


def _attn_out_kernel(mix_ref, qm_ref, km_ref, vm_ref, x_ref, qg_ref, kg_ref, wa_ref, wb_ref, gmean_ref,
                     out_ref):
    qm = qm_ref[0]
    ms = _dot((qm * qm).astype(BF16), gmean_ref[...])
    qn = qm * lax.rsqrt(ms + RMS_EPS) * qg_ref[...]
    kb = (km_ref[0] * kg_ref[...]).astype(BF16)
    vb = vm_ref[0].astype(BF16)
    lane = lax.broadcasted_iota(jnp.int32, qn.shape, 1)
    mem = jnp.zeros_like(qn)
    for hd in range(MEM_HEADS):
        in_head = (lane >= hd * HEAD_DIM) & (lane < (hd + 1) * HEAD_DIM)
        s = _dot_nt(jnp.where(in_head, qn, 0.0).astype(BF16), kb)
        p = jnp.exp(s - jnp.max(s, axis=-1, keepdims=True))
        o = _dot(p.astype(BF16), vb) / jnp.sum(p, axis=-1, keepdims=True)
        mem = jnp.where(in_head, o, mem)
    mix = jnp.concatenate([mix_ref[0, hp] for hp in range(HEAD_PAIRS)], axis=1)
    y = _dot(mix, wa_ref[...]) + _dot(mem.astype(BF16), wb_ref[...])
    out_ref[0] = x_ref[0] + y


def _attn_out(mix, qm, k_mem, v_mem, x, q_g, k_g, w_out, tm):
    B, T, D = x.shape
    M = k_mem.shape[1]
    gmean = _group_matrix(MEM_WIDTH, HEAD_DIM, 1.0 / HEAD_DIM)
    qg = jnp.tile(q_g, MEM_HEADS).reshape(1, MEM_WIDTH) * HEAD_DIM ** -0.5
    kg = jnp.tile(k_g, MEM_HEADS).reshape(1, MEM_WIDTH)
    w = w_out.astype(BF16)
    rows = lambda n: pl.BlockSpec((1, tm, n), lambda b, t: (b, t, 0))
    memspec = pl.BlockSpec((1, M, MEM_WIDTH), lambda b, t: (b, 0, 0))
    return pl.pallas_call(
        _attn_out_kernel,
        grid=(B, T // tm),
        in_specs=[_slab_spec(tm), rows(MEM_WIDTH), memspec, memspec, rows(D),
                  _const_spec((1, MEM_WIDTH)), _const_spec((1, MEM_WIDTH)),
                  _const_spec((MIX_WIDTH, D)), _const_spec((MEM_WIDTH, D)),
                  _const_spec((MEM_WIDTH, MEM_WIDTH))],
        out_specs=rows(D),
        out_shape=jax.ShapeDtypeStruct((B, T, D), F32),
        compiler_params=_params("arbitrary", "arbitrary"),
        name="attn_out",
    )(mix, qm, k_mem, v_mem, x, qg, kg, w[:MIX_WIDTH], w[MIX_WIDTH:], gmean)


def _ffn_kernel(x_ref, ng_ref, w1_ref, w2_ref, out_ref, *, ff_chunk):
    x = x_ref[0]
    h = _rms_rows(x, ng_ref[...]).astype(BF16)
    acc = x
    for c in range(0, w1_ref.shape[1], ff_chunk):
        u = jnp.maximum(_dot(h, w1_ref[:, c:c + ff_chunk]), 0.0)
        acc = acc + _dot((u * u).astype(BF16), w2_ref[c:c + ff_chunk, :])
    out_ref[0] = acc


def _ffn(x, norm_g, w1, w2, tm):
    B, T, D = x.shape
    F = w1.shape[1]
    rows = pl.BlockSpec((1, tm, D), lambda b, t: (b, t, 0))
    return pl.pallas_call(
        functools.partial(_ffn_kernel, ff_chunk=D),
        grid=(B, T // tm),
        in_specs=[rows, _const_spec((1, D)), _const_spec((D, F)), _const_spec((F, D))],
        out_specs=rows,
        out_shape=jax.ShapeDtypeStruct((B, T, D), F32),
        compiler_params=_params("arbitrary", "arbitrary"),
        name="ffn",
    )(x, norm_g.reshape(1, D), w1.astype(BF16), w2.astype(BF16))


def _diff_lambda_init(layer):
    return 0.8 - 0.6 * math.exp(-0.3 * layer)


def kernel(x, mem, norm_mix_g, norm_ffn_g, w_out, w_ff1, w_ff2, mem_norm_g, w_mem_kv, mem_q_norm_g, mem_k_norm_g, rw_in, rw_mu, rw_w0, rw_w2, rw_a0, rw_a2, rw_g2, rw_k_k, rw_k_a, rw_r_k, rw_lnx_g, rw_lnx_b, df_in, df_q_norm_g, df_k_norm_g, df_lq1, df_lk1, df_lq2, df_lk2, df_subln_g):
    T = x.shape[1]
    tm = min(T, 512)
    depth = norm_mix_g.shape[0]
    k_mem, v_mem = _mem_kv(mem, mem_norm_g, w_mem_kv)
    for layer in range(depth):
        j = layer // 2
        if layer % 2 == 0:
            r, k, v, kn, a, cs, bonus, g, qm = _rwkv_in(
                x, norm_mix_g[layer], rw_in[j], rw_mu[j], rw_w0[j], rw_w2[j], rw_a0[j], rw_a2[j],
                rw_g2[j], rw_k_k[j], rw_k_a[j], rw_r_k[j], tm)
            mix = _rwkv_scan(r, k, v, kn, a, cs, bonus, g, rw_lnx_g[j], rw_lnx_b[j], min(T, 512))
        else:
            q, k, v, qm = _diff_in(x, norm_mix_g[layer], df_in[j], df_q_norm_g[j], df_k_norm_g[j],
                                   min(T, 1024))
            mix = _diff_flash(q, k, v, df_lq1[j], df_lk1[j], df_lq2[j], df_lk2[j], df_subln_g[j],
                              _diff_lambda_init(layer), min(T, 512))
        x = _attn_out(mix, qm, k_mem, v_mem, x, mem_q_norm_g[layer], mem_k_norm_g[layer], w_out[layer],
                      min(T, 1024))
        x = _ffn(x, norm_ffn_g[layer], w_ff1[layer], w_ff2[layer], min(T, 1024))
    return x
```

```python
import functools
import math

import jax
import jax.numpy as jnp
from jax import lax
from jax.experimental import pallas as pl
from jax.experimental.pallas import tpu as pltpu

F32 = jnp.float32
BF16 = jnp.bfloat16

D_MODEL = 1024
HEAD_DIM = 64
MIX_WIDTH = 768
MEM_WIDTH = 256
MEM_HEADS = 4
DECAY_LORA = 64
ICLR_LORA = 64
GATE_LORA = 128
RWKV_COLS = 3 * MIX_WIDTH + DECAY_LORA + ICLR_LORA + GATE_LORA
D_FF = 4 * D_MODEL
RMS_EPS = 1e-6
GN_EPS = 64e-5

LANES = 128
HEAD_PAIRS = MIX_WIDTH // LANES
GROUP_COLS = 256
CHUNK = 64
SCAN_PAIRS = 3
NEG_BIG = -1e30
LOG2_E = math.log2(math.e)
FEAT_RADIX = 32
ONES_ROWS = 16
VMEM_LIMIT = 56 * 1024 * 1024


def _dot(a, b):
    return jnp.dot(a, b, preferred_element_type=F32)


def _dot_nt(a, b):
    return lax.dot_general(a, b, (((1,), (1,)), ((), ())), preferred_element_type=F32)


def _dot_split(x, m):
    hi = x.astype(BF16)
    lo = (x - hi.astype(F32)).astype(BF16)
    return _dot(hi, m) + _dot(lo, m)


def _dot_split_left(m, x):
    hi = x.astype(BF16)
    lo = (x - hi.astype(F32)).astype(BF16)
    return _dot(m, hi) + _dot(m, lo)


def _sigmoid(x):
    return 1.0 / (1.0 + jnp.exp(-x))


def _rms_rows(x, g):
    return x * lax.rsqrt(jnp.mean(x * x, axis=-1, keepdims=True) + RMS_EPS) * g


def _group_matrix(n, group, value):
    i = jnp.arange(n) // group
    return jnp.where(i[:, None] == i[None, :], value, 0.0).astype(BF16)


def _params(*sem):
    return pltpu.CompilerParams(dimension_semantics=sem, vmem_limit_bytes=VMEM_LIMIT)


def _const_spec(shape):
    nd = len(shape)
    return pl.BlockSpec(shape, lambda *_: (0,) * nd)


def _slab_spec(tm):
    return pl.BlockSpec((1, HEAD_PAIRS, tm, LANES), lambda b, t: (b, 0, t, 0))


def _mem_kv_kernel(mem_ref, g_ref, w_ref, gm_ref, k_out, v_out):
    h = _rms_rows(mem_ref[0], g_ref[...])
    kv = _dot(h.astype(BF16), w_ref[...])
    k = kv[:, :MEM_WIDTH]
    ms = _dot_split(k * k, gm_ref[...])
    k_out[0] = k * lax.rsqrt(ms + RMS_EPS)
    v_out[0] = kv[:, MEM_WIDTH:]


def _mem_kv(mem, mem_norm_g, w_mem_kv):
    B, M, D = mem.shape
    gm = _group_matrix(MEM_WIDTH, HEAD_DIM, 1.0 / HEAD_DIM)
    out = jax.ShapeDtypeStruct((B, M, MEM_WIDTH), F32)
    return pl.pallas_call(
        _mem_kv_kernel,
        grid=(B,),
        in_specs=[pl.BlockSpec((1, M, D), lambda b: (b, 0, 0)),
                  _const_spec((1, D)), _const_spec((D, 2 * MEM_WIDTH)),
                  _const_spec((MEM_WIDTH, MEM_WIDTH))],
        out_specs=[pl.BlockSpec((1, M, MEM_WIDTH), lambda b: (b, 0, 0))] * 2,
        out_shape=[out, out],
        compiler_params=_params("arbitrary"),
        name="mem_kv",
    )(mem, mem_norm_g.reshape(1, D), w_mem_kv.astype(BF16), gm)


def _rwkv_in_kernel(x_ref, ng_ref, w_ref, mu_ref, w0_ref, w2_ref, a0_ref, a2_ref, g2_ref,
                    kk_ref, ka_ref, rk_ref, gsum_ref, tri_ref,
                    r_out, k_out, v_out, kn_out, a_out, cs_out, bonus_out, g_out, qm_out,
                    carry_ref):
    C = MIX_WIDTH
    tm = x_ref.shape[1]

    @pl.when(pl.program_id(1) == 0)
    def _():
        carry_ref[...] = jnp.zeros_like(carry_ref)

    h = _rms_rows(x_ref[0], ng_ref[...]).astype(BF16)
    proj = _dot(h, w_ref[...])
    cur = proj[:, :RWKV_COLS]
    row = lax.broadcasted_iota(jnp.int32, (tm, 1), 0)
    prev = jnp.where(row == 0, carry_ref[0:1, :], pltpu.roll(cur, 1, axis=0))
    carry_ref[0:1, :] = cur[tm - 1:tm, :]
    slab = cur + (prev - cur) * mu_ref[...]

    r = slab[:, 0:C]
    k = slab[:, C:2 * C]
    v = slab[:, 2 * C:3 * C]
    wa = slab[:, 3 * C:3 * C + DECAY_LORA + ICLR_LORA]
    gd = slab[:, 3 * C + DECAY_LORA + ICLR_LORA:RWKV_COLS]

    d = w0_ref[...] + _dot(jnp.tanh(wa).astype(BF16), w2_ref[...])
    log_decay = -math.exp(-0.5) * _sigmoid(d)
    a = _sigmoid(a0_ref[...] + _dot(wa.astype(BF16), a2_ref[...]))
    g = _dot(_sigmoid(gd).astype(BF16), g2_ref[...])

    gsum = gsum_ref[...]

    def head_sum(t):
        return jnp.concatenate(
            [_dot(t[:, i:i + GROUP_COLS].astype(BF16), gsum) for i in range(0, C, GROUP_COLS)], axis=1)

    kn = k * kk_ref[...]
    kn = kn * lax.rsqrt(jnp.maximum(head_sum(kn * kn), 1e-24))
    k2 = k * (1.0 + (a - 1.0) * ka_ref[...])
    bonus = head_sum(r * k2 * rk_ref[...]) * v

    cs = jnp.concatenate([_dot_split_left(tri_ref[...], log_decay[i:i + GROUP_COLS])
                          for i in range(0, tm, GROUP_COLS)], axis=0)
    outs = ((r_out, r), (k_out, k2), (v_out, v), (kn_out, kn), (a_out, a), (cs_out, cs),
            (bonus_out, bonus), (g_out, g))
    for ref, val in outs:
        for hp in range(HEAD_PAIRS):
            ref[0, hp] = val[:, hp * LANES:(hp + 1) * LANES]
    qm_out[0] = proj[:, RWKV_COLS:]


def _rwkv_in(x, norm_g, w_in, mu, w0, w2, a0, a2, g2, k_k, k_a, r_k, tm):
    B, T, D = x.shape
    C = MIX_WIDTH
    n_in = w_in.shape[1]
    zeros = jnp.zeros((DECAY_LORA, C), F32)
    w2p = jnp.concatenate([w2, zeros], axis=0).astype(BF16)
    a2p = jnp.concatenate([zeros, a2], axis=0).astype(BF16)
    gsum = _group_matrix(GROUP_COLS, HEAD_DIM, 1.0)
    t_idx = jnp.arange(GROUP_COLS)
    tri = ((t_idx[:, None] // CHUNK == t_idx[None, :] // CHUNK)
           & (t_idx[None, :] <= t_idx[:, None])).astype(BF16)
    row = lambda p: p.reshape(1, -1)
    big = jax.ShapeDtypeStruct((B, HEAD_PAIRS, T, LANES), F32)
    return pl.pallas_call(
        _rwkv_in_kernel,
        grid=(B, T // tm),
        in_specs=[pl.BlockSpec((1, tm, D), lambda b, t: (b, t, 0)),
                  _const_spec((1, D)), _const_spec((D, n_in)), _const_spec((1, RWKV_COLS)),
                  _const_spec((1, C)), _const_spec((2 * DECAY_LORA, C)),
                  _const_spec((1, C)), _const_spec((2 * ICLR_LORA, C)),
                  _const_spec((GATE_LORA, C)),
                  _const_spec((1, C)), _const_spec((1, C)), _const_spec((1, C)),
                  _const_spec((GROUP_COLS, GROUP_COLS)), _const_spec((GROUP_COLS, GROUP_COLS))],
        out_specs=[_slab_spec(tm)] * 8 + [pl.BlockSpec((1, tm, MEM_WIDTH), lambda b, t: (b, t, 0))],
        out_shape=[big] * 8 + [jax.ShapeDtypeStruct((B, T, MEM_WIDTH), F32)],
        scratch_shapes=[pltpu.VMEM((8, RWKV_COLS), F32)],
        compiler_params=_params("arbitrary", "arbitrary"),
        name="rwkv_in",
    )(x, row(norm_g), w_in.astype(BF16), row(mu), row(w0), w2p, row(a0), a2p, g2.astype(BF16),
      row(k_k), row(k_a), row(r_k), gsum, tri)


def _rwkv_scan_kernel(r_ref, k_ref, v_ref, kn_ref, a_ref, cs_ref, bonus_ref, g_ref,
                      lng_ref, lnb_ref, gmean_ref, out_ref,
                      s_ref, rhat_ref, y_ref, m_ref, c_ref):
    L = CHUNK
    tc = r_ref.shape[2]
    n_chunks = tc // L

    @pl.when(pl.program_id(1) == 0)
    def _():
        s_ref[...] = jnp.zeros_like(s_ref)

    lane = lax.broadcasted_iota(jnp.int32, (L, LANES), 1)
    row = lax.broadcasted_iota(jnp.int32, (L, LANES), 0)
    head0 = lane < HEAD_DIM
    src = lane & (HEAD_DIM - 1)
    strict = row > src
    incl = row >= src
    r2 = lax.broadcasted_iota(jnp.int32, (LANES, LANES), 0)
    c2 = lax.broadcasted_iota(jnp.int32, (LANES, LANES), 1)
    same_head = (r2 >> 6) == (c2 >> 6)
    eye = r2 == c2
    eye_pk = jnp.where(row == src, 1.0, 0.0)
    first_row = (lax.broadcasted_iota(jnp.int32, (tc, LANES), 0) & (L - 1)) == 0

    def stack2(t):
        return jnp.concatenate([jnp.where(head0, t, 0.0), jnp.where(head0, 0.0, t)], axis=0)

    def side2(x, y):
        return jnp.concatenate([stack2(x), stack2(y)], axis=1).astype(BF16)

    def phase_a(hps):
        cut = lambda t: [t[c * L:(c + 1) * L] for c in range(n_chunks)]
        rt, kt, at, bt, vv, w_last = [], [], [], [], [], []
        for hp in hps:
            kn = kn_ref[0, hp]
            cs = cs_ref[0, hp]
            cs_prev = jnp.where(first_row, 0.0, pltpu.roll(cs, 1, axis=0))
            w_t = jnp.exp(cs)
            w_inv = jnp.exp(-cs)
            rt += cut(r_ref[0, hp] * w_t)
            kt += cut(k_ref[0, hp] * w_inv)
            at += cut(-kn * jnp.exp(cs_prev))
            bt += cut(kn * a_ref[0, hp] * w_inv)
            vv += cut(v_ref[0, hp])
            w_last += [w_t[(c + 1) * L - 1:(c + 1) * L] for c in range(n_chunks)]
        chunks = range(len(hps) * n_chunks)

        sc = [_dot_nt(jnp.concatenate([at[c], rt[c]], axis=0).astype(BF16),
                      jnp.concatenate([stack2(bt[c]), stack2(kt[c])], axis=0).astype(BF16))
              for c in chunks]
        a_ab = [jnp.where(strict, s[:L, :LANES], 0.0) for s in sc]
        a_ak = [jnp.where(strict, s[:L, LANES:], 0.0).astype(BF16) for s in sc]
        a_rb = [jnp.where(incl, s[L:, :LANES], 0.0).astype(BF16) for s in sc]
        a_rk = [jnp.where(incl, s[L:, LANES:], 0.0).astype(BF16) for s in sc]

        npow = a_ab
        inv = [eye_pk + n for n in npow]
        nbd = [stack2(n).astype(BF16) for n in npow]
        npow = [_dot(n.astype(BF16), b) for n, b in zip(npow, nbd)]
        for _ in range(int(math.log2(L)) - 2):
            nbd = [stack2(n).astype(BF16) for n in npow]
            both = [_dot(jnp.concatenate([n, i], axis=0).astype(BF16), b)
                    for n, i, b in zip(npow, inv, nbd)]
            npow = [t[:L] for t in both]
            inv = [i + t[L:] for i, t in zip(inv, both)]
        nbd = [stack2(n).astype(BF16) for n in npow]
        t_pk = [(i + _dot(i.astype(BF16), b)).astype(BF16) for i, b in zip(inv, nbd)]

        v2 = [stack2(t).astype(BF16) for t in vv]
        av = [_dot(jnp.concatenate([a_ak[c], a_rk[c]], axis=0), v2[c]) for c in chunks]
        q0 = [t[:L] for t in av]
        arkv = [t[L:] for t in av]
        pq = [_dot(t_pk[c], side2(at[c], q0[c])) for c in chunks]
        ar = [_dot(a_rb[c], side2(pq[c][:, :LANES], pq[c][:, LANES:])) for c in chunks]
        mc = [_dot(jnp.concatenate([bt[c] * w_last[c], kt[c] * w_last[c]], axis=0).T.astype(BF16),
                   jnp.concatenate(
                       [pq[c], jnp.concatenate([jnp.zeros_like(vv[c]), vv[c]], axis=1)], axis=0).astype(BF16))
              for c in chunks]
        for i in chunks:
            hp, c = hps[i // n_chunks], i % n_chunks
            rows = pl.ds(c * L, L)
            rhat_ref[hp, rows, :] = (rt[i] + ar[i][:, :LANES]).astype(BF16)
            y_ref[hp, rows, :] = ar[i][:, LANES:] + arkv[i]
            m_ref[hp, c] = (jnp.where(same_head, mc[i][:, :LANES], 0.0)
                            + jnp.where(eye, w_last[i], 0.0)).astype(BF16)
            c_ref[hp, c] = jnp.where(same_head, mc[i][:, LANES:], 0.0)

    def phase_b_step(c, hps):
        rows = pl.ds(c * L, L)
        sb = [s_ref[hp].astype(BF16) for hp in hps]
        y = [_dot(rhat_ref[hp, rows, :], b) for hp, b in zip(hps, sb)]
        s_new = [_dot(m_ref[hp, c], b) for hp, b in zip(hps, sb)]
        for i, hp in enumerate(hps):
            y_ref[hp, rows, :] = y_ref[hp, rows, :] + y[i]
            s_ref[hp] = s_new[i] + c_ref[hp, c]

    for g in range(0, HEAD_PAIRS, SCAN_PAIRS):
        phase_a(list(range(g, g + SCAN_PAIRS)))
    for c in range(n_chunks):
        phase_b_step(c, list(range(HEAD_PAIRS)))

    gmean2 = gmean_ref[...]

    def head_mean(t):
        hi = t.astype(BF16)
        lo = (t - hi.astype(F32)).astype(BF16)
        return _dot(jnp.concatenate([hi, lo], axis=1), gmean2)

    pairs = range(HEAD_PAIRS)
    ys = [y_ref[hp] for hp in pairs]
    ds = [y - head_mean(y) for y in ys]
    var = [head_mean(d * d) for d in ds]
    for hp in pairs:
        y = ds[hp] * lax.rsqrt(var[hp] + GN_EPS) * lng_ref[hp] + lnb_ref[hp]
        out_ref[0, hp] = ((y + bonus_ref[0, hp]) * g_ref[0, hp]).astype(BF16)


def _rwkv_scan(r, k, v, kn, a, cs, bonus, g, lnx_g, lnx_b, tc):
    B, HP, T, _ = r.shape
    gmean = _group_matrix(LANES, HEAD_DIM, 1.0 / HEAD_DIM)
    gmean = jnp.concatenate([gmean, gmean], axis=0)
    n_chunks = tc // CHUNK
    return pl.pallas_call(
        _rwkv_scan_kernel,
        grid=(B, T // tc),
        in_specs=[_slab_spec(tc)] * 8 + [_const_spec((HP, 1, LANES))] * 2 + [_const_spec((2 * LANES, LANES))],
        out_specs=_slab_spec(tc),
        out_shape=jax.ShapeDtypeStruct((B, HP, T, LANES), BF16),
        scratch_shapes=[pltpu.VMEM((HP, LANES, LANES), F32),
                        pltpu.VMEM((HP, tc, LANES), BF16),
                        pltpu.VMEM((HP, tc, LANES), F32),
                        pltpu.VMEM((HP, n_chunks, LANES, LANES), BF16),
                        pltpu.VMEM((HP, n_chunks, LANES, LANES), F32)],
        compiler_params=_params("arbitrary", "arbitrary"),
        name="rwkv_scan",
    )(r, k, v, kn, a, cs, bonus, g, lnx_g.reshape(HP, 1, LANES), lnx_b.reshape(HP, 1, LANES), gmean)


def _diff_in_kernel(x_ref, ng_ref, w_ref, qg_ref, kg_ref, gmean_ref, q_out, k_out, v_out, qm_out):
    C = MIX_WIDTH
    h = _rms_rows(x_ref[0], ng_ref[...]).astype(BF16)
    proj = _dot(h, w_ref[...])
    gmean = gmean_ref[...]

    def head_norm(t):
        ms = jnp.concatenate(
            [_dot(jnp.square(t[:, i:i + GROUP_COLS]).astype(BF16), gmean)
             for i in range(0, C, GROUP_COLS)], axis=1)
        return t * lax.rsqrt(ms + RMS_EPS)

    q_out[0] = (head_norm(proj[:, :C]) * qg_ref[...]).astype(BF16)
    k_out[0] = (head_norm(proj[:, C:2 * C]) * kg_ref[...]).astype(BF16)
    v_out[0] = proj[:, 2 * C:3 * C].T.astype(BF16)
    qm_out[0] = proj[:, 3 * C:]


def _diff_in(x, norm_g, w_in, q_g, k_g, tm):
    B, T, D = x.shape
    C = MIX_WIDTH
    n_in = w_in.shape[1]
    gmean = _group_matrix(GROUP_COLS, HEAD_DIM, 1.0 / HEAD_DIM)
    qg = jnp.tile(q_g.reshape(-1), HEAD_PAIRS).reshape(1, C) * (HEAD_DIM ** -0.5 * LOG2_E)
    kg = jnp.tile(k_g.reshape(-1), HEAD_PAIRS).reshape(1, C)
    tile = pl.BlockSpec((1, tm, C), lambda b, t: (b, t, 0))
    big = jax.ShapeDtypeStruct((B, T, C), BF16)
    return pl.pallas_call(
        _diff_in_kernel,
        grid=(B, T // tm),
        in_specs=[pl.BlockSpec((1, tm, D), lambda b, t: (b, t, 0)),
                  _const_spec((1, D)), _const_spec((D, n_in)),
                  _const_spec((1, C)), _const_spec((1, C)), _const_spec((GROUP_COLS, GROUP_COLS))],
        out_specs=[tile, tile, pl.BlockSpec((1, C, tm), lambda b, t: (b, 0, t)),
                   pl.BlockSpec((1, tm, MEM_WIDTH), lambda b, t: (b, t, 0))],
        out_shape=[big, big, jax.ShapeDtypeStruct((B, C, T), BF16),
                   jax.ShapeDtypeStruct((B, T, MEM_WIDTH), F32)],
        compiler_params=_params("arbitrary", "arbitrary"),
        name="diff_in",
    )(x, norm_g.reshape(1, D), w_in.astype(BF16), qg, kg, gmean)


def _diff_flash_kernel(q_ref, k_ref, vt_ref, qfeat_ref, kfeat_ref,
                       lq1_ref, lk1_ref, lq2_ref, lk2_ref, sg_ref,
                       out_ref, sa_ref, sb_ref, *, lambda_init, tq):
    seq = q_ref.shape[1]
    lam = (jnp.exp(jnp.sum(lq1_ref[...] * lk1_ref[...], axis=-1, keepdims=True))
           - jnp.exp(jnp.sum(lq2_ref[...] * lk2_ref[...], axis=-1, keepdims=True)) + lambda_init)
    dim = lax.broadcasted_iota(jnp.int32, (LANES, tq), 0)
    key_loc = lax.broadcasted_iota(jnp.int32, (tq, 1), 0)
    q_loc = lax.broadcasted_iota(jnp.int32, (1, 2 * tq), 1) & (tq - 1)
    qfeat = jnp.concatenate([qfeat_ref[0]] * (2 * tq // LANES), axis=1)
    ones_rows = jnp.ones((ONES_ROWS, tq), BF16)

    def stacked_queries(t):
        qt = q_ref[0, t * tq:(t + 1) * tq, :].astype(F32).T
        zero = jnp.zeros_like(qt)
        top = jnp.concatenate(
            [jnp.where(dim < HEAD_DIM, qt, zero), jnp.where(dim < HEAD_DIM, zero, qt)], axis=1)
        return jnp.concatenate([top.astype(BF16), qfeat], axis=0)

    def scores(qs, h):
        rows = slice(h * tq, (h + 1) * tq)
        return _dot(jnp.concatenate([k_ref[0, rows, :], kfeat_ref[rows, :]], axis=1), qs)

    def softmax_pv(s, h, carry, masked):
        m, acc = carry
        if masked:
            s = jnp.where(key_loc <= q_loc, s, NEG_BIG)
        m_new = jnp.maximum(m, jnp.max(s, axis=0, keepdims=True))
        p = jnp.exp2(s - m_new).astype(BF16)
        vt = vt_ref[0, :, h * tq:(h + 1) * tq]
        pv = _dot(jnp.concatenate([vt, ones_rows], axis=0), p)
        return m_new, jnp.exp2(m - m_new) * acc + pv

    tasks = [(t, h) for t in range(seq // tq) for h in range(t + 1)]
    bufs = (sa_ref, sb_ref)
    fresh = (jnp.full((1, 2 * tq), NEG_BIG, F32), jnp.zeros((LANES + ONES_ROWS, 2 * tq), F32))
    qs = stacked_queries(0)
    bufs[0][...] = scores(qs, 0)
    carry = fresh
    for n, (t, h) in enumerate(tasks):
        if n + 1 < len(tasks):
            t_next, h_next = tasks[n + 1]
            qs_next = qs if t_next == t else stacked_queries(t_next)
            bufs[(n + 1) % 2][...] = scores(qs_next, h_next)
        carry = softmax_pv(bufs[n % 2][...], h, carry, h == t)
        if h == t:
            acc = carry[1]
            o = acc[:LANES] / acc[LANES:LANES + 1]
            o = o[:, :tq] - lam * o[:, tq:]
            o = o * lax.rsqrt(jnp.mean(o * o, axis=0, keepdims=True) + RMS_EPS) * sg_ref[...]
            out_ref[0, 0, t * tq:(t + 1) * tq, :] = (o * (1.0 - lambda_init)).T.astype(BF16)
            carry = fresh
        if n + 1 < len(tasks):
            qs = qs_next


def _diff_flash(q, k, vt, lq1, lk1, lq2, lk2, subln_g, lambda_init, tq):
    B, T, C = q.shape
    H = C // LANES
    slopes = jnp.exp2(-8.0 * jnp.arange(1, H + 1, dtype=F32) / H) * LOG2_E
    assert T // FEAT_RADIX <= 256
    s1 = slopes.astype(BF16)
    r1 = slopes - s1.astype(F32)
    s2 = r1.astype(BF16)
    s3 = (r1 - s2.astype(F32)).astype(BF16)
    pieces = jnp.stack([s1, s2, s3], axis=1).astype(F32)
    qfeat = jnp.concatenate([pieces * FEAT_RADIX, pieces, jnp.zeros((H, LANES - 6), F32)], axis=1)
    qfeat = jnp.broadcast_to(qfeat.astype(BF16)[:, :, None], (H, LANES, LANES))
    pos = jnp.arange(T)
    hi = jnp.broadcast_to((pos // FEAT_RADIX)[:, None], (T, 3))
    lo = jnp.broadcast_to((pos % FEAT_RADIX)[:, None], (T, 3))
    kfeat = jnp.concatenate([hi, lo, jnp.zeros((T, LANES - 6), jnp.int32)], axis=1).astype(BF16)
    lvec = _const_spec((1, HEAD_DIM))
    seq = pl.BlockSpec((1, T, LANES), lambda b, h: (b, 0, h))
    return pl.pallas_call(
        functools.partial(_diff_flash_kernel, lambda_init=lambda_init, tq=tq),
        grid=(B, H),
        in_specs=[seq, seq, pl.BlockSpec((1, LANES, T), lambda b, h: (b, h, 0)),
                  pl.BlockSpec((1, LANES, LANES), lambda b, h: (h, 0, 0)), _const_spec((T, LANES)),
                  lvec, lvec, lvec, lvec, _const_spec((LANES, 1))],
        out_specs=pl.BlockSpec((1, 1, T, LANES), lambda b, h: (b, h, 0, 0)),
        out_shape=jax.ShapeDtypeStruct((B, H, T, LANES), BF16),
        scratch_shapes=[pltpu.VMEM((tq, 2 * tq), F32)] * 2,
        compiler_params=_params("arbitrary", "arbitrary"),
        name="diff_flash",
    )(q, k, vt, qfeat, kfeat,
      lq1.reshape(1, -1), lk1.reshape(1, -1), lq2.reshape(1, -1), lk2.reshape(1, -1),
      subln_g.reshape(LANES, 1))


def _attn_out_kernel(mix_ref, qm_ref, km_ref, vm_ref, x_ref, qg_ref, kg_ref, wa_ref, wb_ref, gmean_ref,
                     out_ref):
    qm = qm_ref[0]
    ms = _dot((qm * qm).astype(BF16), gmean_ref[...])
    qn = qm * lax.rsqrt(ms + RMS_EPS) * qg_ref[...]
    kb = (km_ref[0] * kg_ref[...]).astype(BF16)
    vb = vm_ref[0].astype(BF16)
    lane = lax.broadcasted_iota(jnp.int32, qn.shape, 1)
    mem = jnp.zeros_like(qn)
    for hd in range(MEM_HEADS):
        in_head = (lane >= hd * HEAD_DIM) & (lane < (hd + 1) * HEAD_DIM)
        s = _dot_nt(jnp.where(in_head, qn, 0.0).astype(BF16), kb)
        p = jnp.exp(s - jnp.max(s, axis=-1, keepdims=True))
        o = _dot(p.astype(BF16), vb) / jnp.sum(p, axis=-1, keepdims=True)
        mem = jnp.where(in_head, o, mem)
    mix = jnp.concatenate([mix_ref[0, hp] for hp in range(HEAD_PAIRS)], axis=1)
    y = _dot(mix, wa_ref[...]) + _dot(mem.astype(BF16), wb_ref[...])
    out_ref[0] = x_ref[0] + y


def _attn_out(mix, qm, k_mem, v_mem, x, q_g, k_g, w_out, tm):
    B, T, D = x.shape
    M = k_mem.shape[1]
    gmean = _group_matrix(MEM_WIDTH, HEAD_DIM, 1.0 / HEAD_DIM)
    qg = jnp.tile(q_g, MEM_HEADS).reshape(1, MEM_WIDTH) * HEAD_DIM ** -0.5
    kg = jnp.tile(k_g, MEM_HEADS).reshape(1, MEM_WIDTH)
    w = w_out.astype(BF16)
    rows = lambda n: pl.BlockSpec((1, tm, n), lambda b, t: (b, t, 0))
    memspec = pl.BlockSpec((1, M, MEM_WIDTH), lambda b, t: (b, 0, 0))
    return pl.pallas_call(
        _attn_out_kernel,
        grid=(B, T // tm),
        in_specs=[_slab_spec(tm), rows(MEM_WIDTH), memspec, memspec, rows(D),
                  _const_spec((1, MEM_WIDTH)), _const_spec((1, MEM_WIDTH)),
                  _const_spec((MIX_WIDTH, D)), _const_spec((MEM_WIDTH, D)),
                  _const_spec((MEM_WIDTH, MEM_WIDTH))],
        out_specs=rows(D),
        out_shape=jax.ShapeDtypeStruct((B, T, D), F32),
        compiler_params=_params("arbitrary", "arbitrary"),
        name="attn_out",
    )(mix, qm, k_mem, v_mem, x, qg, kg, w[:MIX_WIDTH], w[MIX_WIDTH:], gmean)


def _ffn_kernel(x_ref, ng_ref, w1_ref, w2_ref, out_ref, *, ff_chunk):
    x = x_ref[0]
    h = _rms_rows(x, ng_ref[...]).astype(BF16)
    acc = x
    for c in range(0, w1_ref.shape[1], ff_chunk):
        u = jnp.maximum(_dot(h, w1_ref[:, c:c + ff_chunk]), 0.0)
        acc = acc + _dot((u * u).astype(BF16), w2_ref[c:c + ff_chunk, :])
    out_ref[0] = acc


def _ffn(x, norm_g, w1, w2, tm):
    B, T, D = x.shape
    F = w1.shape[1]
    rows = pl.BlockSpec((1, tm, D), lambda b, t: (b, t, 0))
    return pl.pallas_call(
        functools.partial(_ffn_kernel, ff_chunk=D),
        grid=(B, T // tm),
        in_specs=[rows, _const_spec((1, D)), _const_spec((D, F)), _const_spec((F, D))],
        out_specs=rows,
        out_shape=jax.ShapeDtypeStruct((B, T, D), F32),
        compiler_params=_params("arbitrary", "arbitrary"),
        name="ffn",
    )(x, norm_g.reshape(1, D), w1.astype(BF16), w2.astype(BF16))


def _diff_lambda_init(layer):
    return 0.8 - 0.6 * math.exp(-0.3 * layer)


def kernel(x, mem, norm_mix_g, norm_ffn_g, w_out, w_ff1, w_ff2, mem_norm_g, w_mem_kv, mem_q_norm_g, mem_k_norm_g, rw_in, rw_mu, rw_w0, rw_w2, rw_a0, rw_a2, rw_g2, rw_k_k, rw_k_a, rw_r_k, rw_lnx_g, rw_lnx_b, df_in, df_q_norm_g, df_k_norm_g, df_lq1, df_lk1, df_lq2, df_lk2, df_subln_g):
    T = x.shape[1]
    tm = min(T, 512)
    depth = norm_mix_g.shape[0]
    k_mem, v_mem = _mem_kv(mem, mem_norm_g, w_mem_kv)
    for layer in range(depth):
        j = layer // 2
        if layer % 2 == 0:
            r, k, v, kn, a, cs, bonus, g, qm = _rwkv_in(
                x, norm_mix_g[layer], rw_in[j], rw_mu[j], rw_w0[j], rw_w2[j], rw_a0[j], rw_a2[j],
                rw_g2[j], rw_k_k[j], rw_k_a[j], rw_r_k[j], tm)
            mix = _rwkv_scan(r, k, v, kn, a, cs, bonus, g, rw_lnx_g[j], rw_lnx_b[j], min(T, 512))
        else:
            q, k, v, qm = _diff_in(x, norm_mix_g[layer], df_in[j], df_q_norm_g[j], df_k_norm_g[j],
                                   min(T, 1024))
            mix = _diff_flash(q, k, v, df_lq1[j], df_lk1[j], df_lq2[j], df_lk2[j], df_subln_g[j],
                              _diff_lambda_init(layer), min(T, 512))
        x = _attn_out(mix, qm, k_mem, v_mem, x, mem_q_norm_g[layer], mem_k_norm_g[layer], w_out[layer],
                      min(T, 1024))
        x = _ffn(x, norm_ffn_g[layer], w_ff1[layer], w_ff2[layer], min(T, 1024))
    return x
```

```python
import functools
import math

import jax
import jax.numpy as jnp
from jax import lax
from jax.experimental import pallas as pl
from jax.experimental.pallas import tpu as pltpu

F32 = jnp.float32
BF16 = jnp.bfloat16

HEAD_DIM = 64
MIX_WIDTH = 768
MEM_WIDTH = 256
MEM_HEADS = 4
DECAY_LORA = 64
ICLR_LORA = 64
GATE_LORA = 128
RWKV_COLS = 3 * MIX_WIDTH + DECAY_LORA + ICLR_LORA + GATE_LORA
RMS_EPS = 1e-6
GN_EPS = 64e-5

LANES = 128
HEAD_PAIRS = MIX_WIDTH // LANES
GROUP_COLS = 256
CHUNK = 64
SCAN_PAIRS = 3
NEG_BIG = -1e30
LOG2_E = math.log2(math.e)
FEAT_RADIX = 32
ONES_ROWS = 16
VMEM_LIMIT = 56 * 1024 * 1024


def _dot(a, b):
    return jnp.dot(a, b, preferred_element_type=F32)


def _dot_nt(a, b):
    return lax.dot_general(a, b, (((1,), (1,)), ((), ())), preferred_element_type=F32)


def _dot_split(x, m):
    hi = x.astype(BF16)
    lo = (x - hi.astype(F32)).astype(BF16)
    return _dot(hi, m) + _dot(lo, m)


def _dot_split_left(m, x):
    hi = x.astype(BF16)
    lo = (x - hi.astype(F32)).astype(BF16)
    return _dot(m, hi) + _dot(m, lo)


def _sigmoid(x):
    return 1.0 / (1.0 + jnp.exp(-x))


def _rms_rows(x, g):
    return x * lax.rsqrt(jnp.mean(x * x, axis=-1, keepdims=True) + RMS_EPS) * g


def _group_matrix(n, group, value):
    i = jnp.arange(n) // group
    return jnp.where(i[:, None] == i[None, :], value, 0.0).astype(BF16)


def _params(*sem):
    return pltpu.CompilerParams(dimension_semantics=sem, vmem_limit_bytes=VMEM_LIMIT)


def _const_spec(shape):
    nd = len(shape)
    return pl.BlockSpec(shape, lambda *_: (0,) * nd)


def _slab_spec(tm):
    return pl.BlockSpec((1, HEAD_PAIRS, tm, LANES), lambda b, t: (b, 0, t, 0))


def _mem_kv_kernel(mem_ref, g_ref, w_ref, gm_ref, k_out, v_out):
    h = _rms_rows(mem_ref[0], g_ref[...])
    kv = _dot(h.astype(BF16), w_ref[...])
    k = kv[:, :MEM_WIDTH]
    ms = _dot_split(k * k, gm_ref[...])
    k_out[0] = k * lax.rsqrt(ms + RMS_EPS)
    v_out[0] = kv[:, MEM_WIDTH:]


def _mem_kv(mem, mem_norm_g, w_mem_kv):
    B, M, D = mem.shape
    gm = _group_matrix(MEM_WIDTH, HEAD_DIM, 1.0 / HEAD_DIM)
    out = jax.ShapeDtypeStruct((B, M, MEM_WIDTH), F32)
    return pl.pallas_call(
        _mem_kv_kernel,
        grid=(B,),
        in_specs=[pl.BlockSpec((1, M, D), lambda b: (b, 0, 0)),
                  _const_spec((1, D)), _const_spec((D, 2 * MEM_WIDTH)),
                  _const_spec((MEM_WIDTH, MEM_WIDTH))],
        out_specs=[pl.BlockSpec((1, M, MEM_WIDTH), lambda b: (b, 0, 0))] * 2,
        out_shape=[out, out],
        compiler_params=_params("arbitrary"),
        name="mem_kv",
    )(mem, mem_norm_g.reshape(1, D), w_mem_kv.astype(BF16), gm)


def _rwkv_in_kernel(x_ref, ng_ref, w_ref, mu_ref, w0_ref, w2_ref, a0_ref, a2_ref, g2_ref,
                    kk_ref, ka_ref, rk_ref, gsum_ref, tri_ref,
                    r_out, k_out, v_out, kn_out, a_out, cs_out, bonus_out, g_out, qm_out,
                    carry_ref):
    C = MIX_WIDTH
    tm = x_ref.shape[1]

    @pl.when(pl.program_id(1) == 0)
    def _():
        carry_ref[...] = jnp.zeros_like(carry_ref)

    h = _rms_rows(x_ref[0], ng_ref[...]).astype(BF16)
    proj = _dot(h, w_ref[...])
    cur = proj[:, :RWKV_COLS]
    row = lax.broadcasted_iota(jnp.int32, (tm, 1), 0)
    prev = jnp.where(row == 0, carry_ref[0:1, :], pltpu.roll(cur, 1, axis=0))
    carry_ref[0:1, :] = cur[tm - 1:tm, :]
    slab = cur + (prev - cur) * mu_ref[...]

    r = slab[:, 0:C]
    k = slab[:, C:2 * C]
    v = slab[:, 2 * C:3 * C]
    wa = slab[:, 3 * C:3 * C + DECAY_LORA + ICLR_LORA]
    gd = slab[:, 3 * C + DECAY_LORA + ICLR_LORA:RWKV_COLS]

    d = w0_ref[...] + _dot(jnp.tanh(wa).astype(BF16), w2_ref[...])
    log_decay = -math.exp(-0.5) * _sigmoid(d)
    a = _sigmoid(a0_ref[...] + _dot(wa.astype(BF16), a2_ref[...]))
    g = _dot(_sigmoid(gd).astype(BF16), g2_ref[...])

    gsum = gsum_ref[...]

    def head_sum(t):
        return jnp.concatenate(
            [_dot(t[:, i:i + GROUP_COLS].astype(BF16), gsum) for i in range(0, C, GROUP_COLS)], axis=1)

    kn = k * kk_ref[...]
    kn = kn * lax.rsqrt(jnp.maximum(head_sum(kn * kn), 1e-24))
    k2 = k * (1.0 + (a - 1.0) * ka_ref[...])
    bonus = head_sum(r * k2 * rk_ref[...]) * v

    cs = jnp.concatenate([_dot_split_left(tri_ref[...], log_decay[i:i + GROUP_COLS])
                          for i in range(0, tm, GROUP_COLS)], axis=0)
    outs = ((r_out, r), (k_out, k2), (v_out, v), (kn_out, kn), (a_out, a), (cs_out, cs),
            (bonus_out, bonus), (g_out, g))
    for ref, val in outs:
        for hp in range(HEAD_PAIRS):
            ref[0, hp] = val[:, hp * LANES:(hp + 1) * LANES]
    qm_out[0] = proj[:, RWKV_COLS:]


def _rwkv_in(x, norm_g, w_in, mu, w0, w2, a0, a2, g2, k_k, k_a, r_k, tm):
    B, T, D = x.shape
    C = MIX_WIDTH
    n_in = w_in.shape[1]
    zeros = jnp.zeros((DECAY_LORA, C), F32)
    w2p = jnp.concatenate([w2, zeros], axis=0).astype(BF16)
    a2p = jnp.concatenate([zeros, a2], axis=0).astype(BF16)
    gsum = _group_matrix(GROUP_COLS, HEAD_DIM, 1.0)
    t_idx = jnp.arange(GROUP_COLS)
    tri = ((t_idx[:, None] // CHUNK == t_idx[None, :] // CHUNK)
           & (t_idx[None, :] <= t_idx[:, None])).astype(BF16)
    row = lambda p: p.reshape(1, -1)
    big = jax.ShapeDtypeStruct((B, HEAD_PAIRS, T, LANES), F32)
    return pl.pallas_call(
        _rwkv_in_kernel,
        grid=(B, T // tm),
        in_specs=[pl.BlockSpec((1, tm, D), lambda b, t: (b, t, 0)),
                  _const_spec((1, D)), _const_spec((D, n_in)), _const_spec((1, RWKV_COLS)),
                  _const_spec((1, C)), _const_spec((2 * DECAY_LORA, C)),
                  _const_spec((1, C)), _const_spec((2 * ICLR_LORA, C)),
                  _const_spec((GATE_LORA, C)),
                  _const_spec((1, C)), _const_spec((1, C)), _const_spec((1, C)),
                  _const_spec((GROUP_COLS, GROUP_COLS)), _const_spec((GROUP_COLS, GROUP_COLS))],
        out_specs=[_slab_spec(tm)] * 8 + [pl.BlockSpec((1, tm, MEM_WIDTH), lambda b, t: (b, t, 0))],
        out_shape=[big] * 8 + [jax.ShapeDtypeStruct((B, T, MEM_WIDTH), F32)],
        scratch_shapes=[pltpu.VMEM((8, RWKV_COLS), F32)],
        compiler_params=_params("arbitrary", "arbitrary"),
        name="rwkv_in",
    )(x, row(norm_g), w_in.astype(BF16), row(mu), row(w0), w2p, row(a0), a2p, g2.astype(BF16),
      row(k_k), row(k_a), row(r_k), gsum, tri)


def _rwkv_scan_kernel(r_ref, k_ref, v_ref, kn_ref, a_ref, cs_ref, bonus_ref, g_ref,
                      lng_ref, lnb_ref, gmean_ref, out_ref,
                      s_ref, rhat_ref, y_ref, m_ref, c_ref):
    L = CHUNK
    tc = r_ref.shape[2]
    n_chunks = tc // L

    @pl.when(pl.program_id(1) == 0)
    def _():
        s_ref[...] = jnp.zeros_like(s_ref)

    lane = lax.broadcasted_iota(jnp.int32, (L, LANES), 1)
    row = lax.broadcasted_iota(jnp.int32, (L, LANES), 0)
    head0 = lane < HEAD_DIM
    src = lane & (HEAD_DIM - 1)
    strict = row > src
    incl = row >= src
    r2 = lax.broadcasted_iota(jnp.int32, (LANES, LANES), 0)
    c2 = lax.broadcasted_iota(jnp.int32, (LANES, LANES), 1)
    same_head = (r2 >> 6) == (c2 >> 6)
    eye = r2 == c2
    eye_pk = jnp.where(row == src, 1.0, 0.0)
    first_row = (lax.broadcasted_iota(jnp.int32, (tc, LANES), 0) & (L - 1)) == 0

    def stack2(t):
        return jnp.concatenate([jnp.where(head0, t, 0.0), jnp.where(head0, 0.0, t)], axis=0)

    def side2(x, y):
        return jnp.concatenate([stack2(x), stack2(y)], axis=1).astype(BF16)

    def phase_a(hps):
        cut = lambda t: [t[c * L:(c + 1) * L] for c in range(n_chunks)]
        rt, kt, at, bt, vv, w_last = [], [], [], [], [], []
        for hp in hps:
            kn = kn_ref[0, hp]
            cs = cs_ref[0, hp]
            cs_prev = jnp.where(first_row, 0.0, pltpu.roll(cs, 1, axis=0))
            w_t = jnp.exp(cs)
            w_inv = jnp.exp(-cs)
            rt += cut(r_ref[0, hp] * w_t)
            kt += cut(k_ref[0, hp] * w_inv)
            at += cut(-kn * jnp.exp(cs_prev))
            bt += cut(kn * a_ref[0, hp] * w_inv)
            vv += cut(v_ref[0, hp])
            w_last += [w_t[(c + 1) * L - 1:(c + 1) * L] for c in range(n_chunks)]
        chunks = range(len(hps) * n_chunks)

        sc = [_dot_nt(jnp.concatenate([at[c], rt[c]], axis=0).astype(BF16),
                      jnp.concatenate([stack2(bt[c]), stack2(kt[c])], axis=0).astype(BF16))
              for c in chunks]
        a_ab = [jnp.where(strict, s[:L, :LANES], 0.0) for s in sc]
        a_ak = [jnp.where(strict, s[:L, LANES:], 0.0).astype(BF16) for s in sc]
        a_rb = [jnp.where(incl, s[L:, :LANES], 0.0).astype(BF16) for s in sc]
        a_rk = [jnp.where(incl, s[L:, LANES:], 0.0).astype(BF16) for s in sc]

        npow = a_ab
        inv = [eye_pk + n for n in npow]
        nbd = [stack2(n).astype(BF16) for n in npow]
        npow = [_dot(n.astype(BF16), b) for n, b in zip(npow, nbd)]
        for _ in range(int(math.log2(L)) - 2):
            nbd = [stack2(n).astype(BF16) for n in npow]
            both = [_dot(jnp.concatenate([n, i], axis=0).astype(BF16), b)
                    for n, i, b in zip(npow, inv, nbd)]
            npow = [t[:L] for t in both]
            inv = [i + t[L:] for i, t in zip(inv, both)]
        nbd = [stack2(n).astype(BF16) for n in npow]
        t_pk = [(i + _dot(i.astype(BF16), b)).astype(BF16) for i, b in zip(inv, nbd)]

        v2 = [stack2(t).astype(BF16) for t in vv]
        av = [_dot(jnp.concatenate([a_ak[c], a_rk[c]], axis=0), v2[c]) for c in chunks]
        q0 = [t[:L] for t in av]
        arkv = [t[L:] for t in av]
        pq = [_dot(t_pk[c], side2(at[c], q0[c])) for c in chunks]
        ar = [_dot(a_rb[c], side2(pq[c][:, :LANES], pq[c][:, LANES:])) for c in chunks]
        mc = [_dot(jnp.concatenate([bt[c] * w_last[c], kt[c] * w_last[c]], axis=0).T.astype(BF16),
                   jnp.concatenate(
                       [pq[c], jnp.concatenate([jnp.zeros_like(vv[c]), vv[c]], axis=1)], axis=0).astype(BF16))
              for c in chunks]
        for i in chunks:
            hp, c = hps[i // n_chunks], i % n_chunks
            rows = pl.ds(c * L, L)
            rhat_ref[hp, rows, :] = (rt[i] + ar[i][:, :LANES]).astype(BF16)
            y_ref[hp, rows, :] = ar[i][:, LANES:] + arkv[i]
            m_ref[hp, c] = (jnp.where(same_head, mc[i][:, :LANES], 0.0)
                            + jnp.where(eye, w_last[i], 0.0)).astype(BF16)
            c_ref[hp, c] = jnp.where(same_head, mc[i][:, LANES:], 0.0)

    def phase_b_step(c, hps):
        rows = pl.ds(c * L, L)
        sb = [s_ref[hp].astype(BF16) for hp in hps]
        y = [_dot(rhat_ref[hp, rows, :], b) for hp, b in zip(hps, sb)]
        s_new = [_dot(m_ref[hp, c], b) for hp, b in zip(hps, sb)]
        for i, hp in enumerate(hps):
            y_ref[hp, rows, :] = y_ref[hp, rows, :] + y[i]
            s_ref[hp] = s_new[i] + c_ref[hp, c]

    for g in range(0, HEAD_PAIRS, SCAN_PAIRS):
        phase_a(list(range(g, g + SCAN_PAIRS)))
    for c in range(n_chunks):
        phase_b_step(c, list(range(HEAD_PAIRS)))

    gmean2 = gmean_ref[...]

    def head_mean(t):
        hi = t.astype(BF16)
        lo = (t - hi.astype(F32)).astype(BF16)
        return _dot(jnp.concatenate([hi, lo], axis=1), gmean2)

    pairs = range(HEAD_PAIRS)
    ys = [y_ref[hp] for hp in pairs]
    ds = [y - head_mean(y) for y in ys]
    var = [head_mean(d * d) for d in ds]
    for hp in pairs:
        y = ds[hp] * lax.rsqrt(var[hp] + GN_EPS) * lng_ref[hp] + lnb_ref[hp]
        out_ref[0, hp] = ((y + bonus_ref[0, hp]) * g_ref[0, hp]).astype(BF16)


def _rwkv_scan(r, k, v, kn, a, cs, bonus, g, lnx_g, lnx_b, tc):
    B, HP, T, _ = r.shape
    gmean = _group_matrix(LANES, HEAD_DIM, 1.0 / HEAD_DIM)
    gmean = jnp.concatenate([gmean, gmean], axis=0)
    n_chunks = tc // CHUNK
    return pl.pallas_call(
        _rwkv_scan_kernel,
        grid=(B, T // tc),
        in_specs=[_slab_spec(tc)] * 8 + [_const_spec((HP, 1, LANES))] * 2 + [_const_spec((2 * LANES, LANES))],
        out_specs=_slab_spec(tc),
        out_shape=jax.ShapeDtypeStruct((B, HP, T, LANES), BF16),
        scratch_shapes=[pltpu.VMEM((HP, LANES, LANES), F32),
                        pltpu.VMEM((HP, tc, LANES), BF16),
                        pltpu.VMEM((HP, tc, LANES), F32),
                        pltpu.VMEM((HP, n_chunks, LANES, LANES), BF16),
                        pltpu.VMEM((HP, n_chunks, LANES, LANES), F32)],
        compiler_params=_params("arbitrary", "arbitrary"),
        name="rwkv_scan",
    )(r, k, v, kn, a, cs, bonus, g, lnx_g.reshape(HP, 1, LANES), lnx_b.reshape(HP, 1, LANES), gmean)


def _diff_in_kernel(x_ref, ng_ref, w_ref, qg_ref, kg_ref, gmean_ref, q_out, k_out, v_out, qm_out):
    C = MIX_WIDTH
    h = _rms_rows(x_ref[0], ng_ref[...]).astype(BF16)
    proj = _dot(h, w_ref[...])
    gmean = gmean_ref[...]

    def head_norm(t):
        ms = jnp.concatenate(
            [_dot(jnp.square(t[:, i:i + GROUP_COLS]).astype(BF16), gmean)
             for i in range(0, C, GROUP_COLS)], axis=1)
        return t * lax.rsqrt(ms + RMS_EPS)

    q_out[0] = (head_norm(proj[:, :C]) * qg_ref[...]).astype(BF16)
    k_out[0] = (head_norm(proj[:, C:2 * C]) * kg_ref[...]).astype(BF16)
    v_out[0] = proj[:, 2 * C:3 * C].T.astype(BF16)
    qm_out[0] = proj[:, 3 * C:]


def _diff_in(x, norm_g, w_in, q_g, k_g, tm):
    B, T, D = x.shape
    C = MIX_WIDTH
    n_in = w_in.shape[1]
    gmean = _group_matrix(GROUP_COLS, HEAD_DIM, 1.0 / HEAD_DIM)
    qg = jnp.tile(q_g.reshape(-1), HEAD_PAIRS).reshape(1, C) * (HEAD_DIM ** -0.5 * LOG2_E)
    kg = jnp.tile(k_g.reshape(-1), HEAD_PAIRS).reshape(1, C)
    tile = pl.BlockSpec((1, tm, C), lambda b, t: (b, t, 0))
    big = jax.ShapeDtypeStruct((B, T, C), BF16)
    return pl.pallas_call(
        _diff_in_kernel,
        grid=(B, T // tm),
        in_specs=[pl.BlockSpec((1, tm, D), lambda b, t: (b, t, 0)),
                  _const_spec((1, D)), _const_spec((D, n_in)),
                  _const_spec((1, C)), _const_spec((1, C)), _const_spec((GROUP_COLS, GROUP_COLS))],
        out_specs=[tile, tile, pl.BlockSpec((1, C, tm), lambda b, t: (b, 0, t)),
                   pl.BlockSpec((1, tm, MEM_WIDTH), lambda b, t: (b, t, 0))],
        out_shape=[big, big, jax.ShapeDtypeStruct((B, C, T), BF16),
                   jax.ShapeDtypeStruct((B, T, MEM_WIDTH), F32)],
        compiler_params=_params("arbitrary", "arbitrary"),
        name="diff_in",
    )(x, norm_g.reshape(1, D), w_in.astype(BF16), qg, kg, gmean)


def _diff_flash_kernel(q_ref, k_ref, vt_ref, qfeat_ref, kfeat_ref,
                       lq1_ref, lk1_ref, lq2_ref, lk2_ref, sg_ref,
                       out_ref, sa_ref, sb_ref, *, lambda_init, tq):
    seq = q_ref.shape[1]
    lam = (jnp.exp(jnp.sum(lq1_ref[...] * lk1_ref[...], axis=-1, keepdims=True))
           - jnp.exp(jnp.sum(lq2_ref[...] * lk2_ref[...], axis=-1, keepdims=True)) + lambda_init)
    dim = lax.broadcasted_iota(jnp.int32, (LANES, tq), 0)
    key_loc = lax.broadcasted_iota(jnp.int32, (tq, 1), 0)
    q_loc = lax.broadcasted_iota(jnp.int32, (1, 2 * tq), 1) & (tq - 1)
    qfeat = jnp.concatenate([qfeat_ref[0]] * (2 * tq // LANES), axis=1)
    ones_rows = jnp.ones((ONES_ROWS, tq), BF16)

    def stacked_queries(t):
        qt = q_ref[0, t * tq:(t + 1) * tq, :].astype(F32).T
        zero = jnp.zeros_like(qt)
        top = jnp.concatenate(
            [jnp.where(dim < HEAD_DIM, qt, zero), jnp.where(dim < HEAD_DIM, zero, qt)], axis=1)
        return jnp.concatenate([top.astype(BF16), qfeat], axis=0)

    def scores(qs, h):
        rows = slice(h * tq, (h + 1) * tq)
        return _dot(jnp.concatenate([k_ref[0, rows, :], kfeat_ref[rows, :]], axis=1), qs)

    def softmax_pv(s, h, carry, masked):
        m, acc = carry
        if masked:
            s = jnp.where(key_loc <= q_loc, s, NEG_BIG)
        m_new = jnp.maximum(m, jnp.max(s, axis=0, keepdims=True))
        p = jnp.exp2(s - m_new).astype(BF16)
        vt = vt_ref[0, :, h * tq:(h + 1) * tq]
        pv = _dot(jnp.concatenate([vt, ones_rows], axis=0), p)
        return m_new, jnp.exp2(m - m_new) * acc + pv

    tasks = [(t, h) for t in range(seq // tq) for h in range(t + 1)]
    bufs = (sa_ref, sb_ref)
    fresh = (jnp.full((1, 2 * tq), NEG_BIG, F32), jnp.zeros((LANES + ONES_ROWS, 2 * tq), F32))
    qs = stacked_queries(0)
    bufs[0][...] = scores(qs, 0)
    carry = fresh
    for n, (t, h) in enumerate(tasks):
        if n + 1 < len(tasks):
            t_next, h_next = tasks[n + 1]
            qs_next = qs if t_next == t else stacked_queries(t_next)
            bufs[(n + 1) % 2][...] = scores(qs_next, h_next)
        carry = softmax_pv(bufs[n % 2][...], h, carry, h == t)
        if h == t:
            acc = carry[1]
            o = acc[:LANES] / acc[LANES:LANES + 1]
            o = o[:, :tq] - lam * o[:, tq:]
            o = o * lax.rsqrt(jnp.mean(o * o, axis=0, keepdims=True) + RMS_EPS) * sg_ref[...]
            out_ref[0, 0, t * tq:(t + 1) * tq, :] = (o * (1.0 - lambda_init)).T.astype(BF16)
            carry = fresh
        if n + 1 < len(tasks):
            qs = qs_next


def _diff_flash(q, k, vt, lq1, lk1, lq2, lk2, subln_g, lambda_init, tq):
    B, T, C = q.shape
    H = C // LANES
    slopes = jnp.exp2(-8.0 * jnp.arange(1, H + 1, dtype=F32) / H) * LOG2_E
    assert T // FEAT_RADIX <= 256
    s1 = slopes.astype(BF16)
    r1 = slopes - s1.astype(F32)
    s2 = r1.astype(BF16)
    s3 = (r1 - s2.astype(F32)).astype(BF16)
    pieces = jnp.stack([s1, s2, s3], axis=1).astype(F32)
    qfeat = jnp.concatenate([pieces * FEAT_RADIX, pieces, jnp.zeros((H, LANES - 6), F32)], axis=1)
    qfeat = jnp.broadcast_to(qfeat.astype(BF16)[:, :, None], (H, LANES, LANES))
    pos = jnp.arange(T)
    hi = jnp.broadcast_to((pos // FEAT_RADIX)[:, None], (T, 3))
    lo = jnp.broadcast_to((pos % FEAT_RADIX)[:, None], (T, 3))
    kfeat = jnp.concatenate([hi, lo, jnp.zeros((T, LANES - 6), jnp.int32)], axis=1).astype(BF16)
    lvec = _const_spec((1, HEAD_DIM))
    seq = pl.BlockSpec((1, T, LANES), lambda b, h: (b, 0, h))
    return pl.pallas_call(
        functools.partial(_diff_flash_kernel, lambda_init=lambda_init, tq=tq),
        grid=(B, H),
        in_specs=[seq, seq, pl.BlockSpec((1, LANES, T), lambda b, h: (b, h, 0)),
                  pl.BlockSpec((1, LANES, LANES), lambda b, h: (h, 0, 0)), _const_spec((T, LANES)),
                  lvec, lvec, lvec, lvec, _const_spec((LANES, 1))],
        out_specs=pl.BlockSpec((1, 1, T, LANES), lambda b, h: (b, h, 0, 0)),
        out_shape=jax.ShapeDtypeStruct((B, H, T, LANES), BF16),
        scratch_shapes=[pltpu.VMEM((tq, 2 * tq), F32)] * 2,
        compiler_params=_params("arbitrary", "arbitrary"),
        name="diff_flash",
    )(q, k, vt, qfeat, kfeat,
      lq1.reshape(1, -1), lk1.reshape(1, -1), lq2.reshape(1, -1), lk2.reshape(1, -1),
      subln_g.reshape(LANES, 1))


def _attn_out_kernel(mix_ref, qm_ref, km_ref, vm_ref, x_ref, qg_ref, kg_ref, wa_ref, wb_ref, gmean_ref,
                     out_ref):
    qm = qm_ref[0]
    ms = _dot((qm * qm).astype(BF16), gmean_ref[...])
    qn = qm * lax.rsqrt(ms + RMS_EPS) * qg_ref[...]
    kb = (km_ref[0] * kg_ref[...]).astype(BF16)
    vb = vm_ref[0].astype(BF16)
    lane = lax.broadcasted_iota(jnp.int32, qn.shape, 1)
    mem = jnp.zeros_like(qn)
    for hd in range(MEM_HEADS):
        in_head = (lane >= hd * HEAD_DIM) & (lane < (hd + 1) * HEAD_DIM)
        s = _dot_nt(jnp.where(in_head, qn, 0.0).astype(BF16), kb)
        p = jnp.exp(s - jnp.max(s, axis=-1, keepdims=True))
        o = _dot(p.astype(BF16), vb) / jnp.sum(p, axis=-1, keepdims=True)
        mem = jnp.where(in_head, o, mem)
    mix = jnp.concatenate([mix_ref[0, hp] for hp in range(HEAD_PAIRS)], axis=1)
    y = _dot(mix, wa_ref[...]) + _dot(mem.astype(BF16), wb_ref[...])
    out_ref[0] = x_ref[0] + y


def _attn_out(mix, qm, k_mem, v_mem, x, q_g, k_g, w_out, tm):
    B, T, D = x.shape
    M = k_mem.shape[1]
    gmean = _group_matrix(MEM_WIDTH, HEAD_DIM, 1.0 / HEAD_DIM)
    qg = jnp.tile(q_g, MEM_HEADS).reshape(1, MEM_WIDTH) * HEAD_DIM ** -0.5
    kg = jnp.tile(k_g, MEM_HEADS).reshape(1, MEM_WIDTH)
    w = w_out.astype(BF16)
    rows = lambda n: pl.BlockSpec((1, tm, n), lambda b, t: (b, t, 0))
    memspec = pl.BlockSpec((1, M, MEM_WIDTH), lambda b, t: (b, 0, 0))
    return pl.pallas_call(
        _attn_out_kernel,
        grid=(B, T // tm),
        in_specs=[_slab_spec(tm), rows(MEM_WIDTH), memspec, memspec, rows(D),
                  _const_spec((1, MEM_WIDTH)), _const_spec((1, MEM_WIDTH)),
                  _const_spec((MIX_WIDTH, D)), _const_spec((MEM_WIDTH, D)),
                  _const_spec((MEM_WIDTH, MEM_WIDTH))],
        out_specs=rows(D),
        out_shape=jax.ShapeDtypeStruct((B, T, D), F32),
        compiler_params=_params("arbitrary", "arbitrary"),
        name="attn_out",
    )(mix, qm, k_mem, v_mem, x, qg, kg, w[:MIX_WIDTH], w[MIX_WIDTH:], gmean)


def _ffn_kernel(x_ref, ng_ref, w1_ref, w2_ref, out_ref, *, ff_chunk):
    x = x_ref[0]
    h = _rms_rows(x, ng_ref[...]).astype(BF16)
    acc = x
    for c in range(0, w1_ref.shape[1], ff_chunk):
        u = jnp.maximum(_dot(h, w1_ref[:, c:c + ff_chunk]), 0.0)
        acc = acc + _dot((u * u).astype(BF16), w2_ref[c:c + ff_chunk, :])
    out_ref[0] = acc


def _ffn(x, norm_g, w1, w2, tm):
    B, T, D = x.shape
    F = w1.shape[1]
    rows = pl.BlockSpec((1, tm, D), lambda b, t: (b, t, 0))
    return pl.pallas_call(
        functools.partial(_ffn_kernel, ff_chunk=D),
        grid=(B, T // tm),
        in_specs=[rows, _const_spec((1, D)), _const_spec((D, F)), _const_spec((F, D))],
        out_specs=rows,
        out_shape=jax.ShapeDtypeStruct((B, T, D), F32),
        compiler_params=_params("arbitrary", "arbitrary"),
        name="ffn",
    )(x, norm_g.reshape(1, D), w1.astype(BF16), w2.astype(BF16))


def _diff_lambda_init(layer):
    return 0.8 - 0.6 * math.exp(-0.3 * layer)


def _row_tiles(seq):
    tiles = {"rwkv_in": 512, "rwkv_scan": 512, "diff_in": 1024, "diff_flash": 512, "attn_out": 1024, "ffn": 1024}
    tiles = {name: min(seq, rows) for name, rows in tiles.items()}
    assert all(seq % rows == 0 for rows in tiles.values())
    assert tiles["rwkv_in"] % GROUP_COLS == 0 and tiles["rwkv_scan"] % CHUNK == 0
    return tiles


def kernel(x, mem, norm_mix_g, norm_ffn_g, w_out, w_ff1, w_ff2, mem_norm_g, w_mem_kv, mem_q_norm_g, mem_k_norm_g, rw_in, rw_mu, rw_w0, rw_w2, rw_a0, rw_a2, rw_g2, rw_k_k, rw_k_a, rw_r_k, rw_lnx_g, rw_lnx_b, df_in, df_q_norm_g, df_k_norm_g, df_lq1, df_lk1, df_lq2, df_lk2, df_subln_g):
    tiles = _row_tiles(x.shape[1])
    depth = norm_mix_g.shape[0]
    k_mem, v_mem = _mem_kv(mem, mem_norm_g, w_mem_kv)
    for layer in range(depth):
        j = layer // 2
        if layer % 2 == 0:
            r, k, v, kn, a, cs, bonus, g, qm = _rwkv_in(
                x, norm_mix_g[layer], rw_in[j], rw_mu[j], rw_w0[j], rw_w2[j], rw_a0[j], rw_a2[j],
                rw_g2[j], rw_k_k[j], rw_k_a[j], rw_r_k[j], tiles["rwkv_in"])
            mix = _rwkv_scan(r, k, v, kn, a, cs, bonus, g, rw_lnx_g[j], rw_lnx_b[j], tiles["rwkv_scan"])
        else:
            q, k, v, qm = _diff_in(x, norm_mix_g[layer], df_in[j], df_q_norm_g[j], df_k_norm_g[j],
                                   tiles["diff_in"])
            mix = _diff_flash(q, k, v, df_lq1[j], df_lk1[j], df_lq2[j], df_lk2[j], df_subln_g[j],
                              _diff_lambda_init(layer), tiles["diff_flash"])
        x = _attn_out(mix, qm, k_mem, v_mem, x, mem_q_norm_g[layer], mem_k_norm_g[layer], w_out[layer],
                      tiles["attn_out"])
        x = _ffn(x, norm_ffn_g[layer], w_ff1[layer], w_ff2[layer], tiles["ffn"])
    return x
```

```python
import functools
import math

import jax
import jax.numpy as jnp
from jax import lax
from jax.experimental import pallas as pl
from jax.experimental.pallas import tpu as pltpu

F32 = jnp.float32
BF16 = jnp.bfloat16

HEAD_DIM = 64
MIX_WIDTH = 768
MEM_WIDTH = 256
MEM_HEADS = 4
DECAY_LORA = 64
ICLR_LORA = 64
GATE_LORA = 128
RWKV_COLS = 3 * MIX_WIDTH + DECAY_LORA + ICLR_LORA + GATE_LORA
RMS_EPS = 1e-6
GN_EPS = 64e-5

LANES = 128
HEAD_PAIRS = MIX_WIDTH // LANES
GROUP_COLS = 256
CHUNK = 64
SCAN_PAIRS = 3
NEG_BIG = -1e30
LOG2_E = math.log2(math.e)
FEAT_RADIX = 32
ONES_ROWS = 16
VMEM_LIMIT = 56 * 1024 * 1024


def _dot(a, b):
    return jnp.dot(a, b, preferred_element_type=F32)


def _dot_nt(a, b):
    return lax.dot_general(a, b, (((1,), (1,)), ((), ())), preferred_element_type=F32)


def _dot_split(x, m):
    hi = x.astype(BF16)
    lo = (x - hi.astype(F32)).astype(BF16)
    return _dot(hi, m) + _dot(lo, m)


def _dot_split_left(m, x):
    hi = x.astype(BF16)
    lo = (x - hi.astype(F32)).astype(BF16)
    return _dot(m, hi) + _dot(m, lo)


def _sigmoid(x):
    return 1.0 / (1.0 + jnp.exp(-x))


def _rms_rows(x, g):
    return x * lax.rsqrt(jnp.mean(x * x, axis=-1, keepdims=True) + RMS_EPS) * g


def _group_matrix(n, group, value):
    i = jnp.arange(n) // group
    return jnp.where(i[:, None] == i[None, :], value, 0.0).astype(BF16)


def _params(*sem):
    return pltpu.CompilerParams(dimension_semantics=sem, vmem_limit_bytes=VMEM_LIMIT)


def _const_spec(shape):
    nd = len(shape)
    return pl.BlockSpec(shape, lambda *_: (0,) * nd)


def _slab_spec(tm):
    return pl.BlockSpec((1, HEAD_PAIRS, tm, LANES), lambda b, t: (b, 0, t, 0))


def _mem_kv_kernel(mem_ref, g_ref, w_ref, gm_ref, k_out, v_out):
    h = _rms_rows(mem_ref[0], g_ref[...])
    kv = _dot(h.astype(BF16), w_ref[...])
    k = kv[:, :MEM_WIDTH]
    ms = _dot_split(k * k, gm_ref[...])
    k_out[0] = k * lax.rsqrt(ms + RMS_EPS)
    v_out[0] = kv[:, MEM_WIDTH:]


def _mem_kv(mem, mem_norm_g, w_mem_kv):
    B, M, D = mem.shape
    gm = _group_matrix(MEM_WIDTH, HEAD_DIM, 1.0 / HEAD_DIM)
    out = jax.ShapeDtypeStruct((B, M, MEM_WIDTH), F32)
    return pl.pallas_call(
        _mem_kv_kernel,
        grid=(B,),
        in_specs=[pl.BlockSpec((1, M, D), lambda b: (b, 0, 0)),
                  _const_spec((1, D)), _const_spec((D, 2 * MEM_WIDTH)),
                  _const_spec((MEM_WIDTH, MEM_WIDTH))],
        out_specs=[pl.BlockSpec((1, M, MEM_WIDTH), lambda b: (b, 0, 0))] * 2,
        out_shape=[out, out],
        compiler_params=_params("arbitrary"),
        name="mem_kv",
    )(mem, mem_norm_g.reshape(1, D), w_mem_kv.astype(BF16), gm)


def _rwkv_in_kernel(x_ref, ng_ref, w_ref, mu_ref, w0_ref, w2_ref, a0_ref, a2_ref, g2_ref,
                    kk_ref, ka_ref, rk_ref, gsum_ref, tri_ref,
                    r_out, k_out, v_out, kn_out, a_out, cs_out, bonus_out, g_out, qm_out,
                    carry_ref):
    C = MIX_WIDTH
    tm = x_ref.shape[1]

    @pl.when(pl.program_id(1) == 0)
    def _():
        carry_ref[...] = jnp.zeros_like(carry_ref)

    h = _rms_rows(x_ref[0], ng_ref[...]).astype(BF16)
    proj = _dot(h, w_ref[...])
    cur = proj[:, :RWKV_COLS]
    row = lax.broadcasted_iota(jnp.int32, (tm, 1), 0)
    prev = jnp.where(row == 0, carry_ref[0:1, :], pltpu.roll(cur, 1, axis=0))
    carry_ref[0:1, :] = cur[tm - 1:tm, :]
    slab = cur + (prev - cur) * mu_ref[...]

    r = slab[:, 0:C]
    k = slab[:, C:2 * C]
    v = slab[:, 2 * C:3 * C]
    wa = slab[:, 3 * C:3 * C + DECAY_LORA + ICLR_LORA]
    gd = slab[:, 3 * C + DECAY_LORA + ICLR_LORA:RWKV_COLS]

    d = w0_ref[...] + _dot(jnp.tanh(wa).astype(BF16), w2_ref[...])
    log_decay = -math.exp(-0.5) * _sigmoid(d)
    a = _sigmoid(a0_ref[...] + _dot(wa.astype(BF16), a2_ref[...]))
    g = _dot(_sigmoid(gd).astype(BF16), g2_ref[...])

    gsum = gsum_ref[...]

    def head_sum(t):
        return jnp.concatenate(
            [_dot(t[:, i:i + GROUP_COLS].astype(BF16), gsum) for i in range(0, C, GROUP_COLS)], axis=1)

    kn = k * kk_ref[...]
    kn = kn * lax.rsqrt(jnp.maximum(head_sum(kn * kn), 1e-24))
    k2 = k * (1.0 + (a - 1.0) * ka_ref[...])
    bonus = head_sum(r * k2 * rk_ref[...]) * v

    cs = jnp.concatenate([_dot_split_left(tri_ref[...], log_decay[i:i + GROUP_COLS])
                          for i in range(0, tm, GROUP_COLS)], axis=0)
    outs = ((r_out, r), (k_out, k2), (v_out, v), (kn_out, kn), (a_out, a), (cs_out, cs),
            (bonus_out, bonus), (g_out, g))
    for ref, val in outs:
        for hp in range(HEAD_PAIRS):
            ref[0, hp] = val[:, hp * LANES:(hp + 1) * LANES]
    qm_out[0] = proj[:, RWKV_COLS:]


def _rwkv_in(x, norm_g, w_in, mu, w0, w2, a0, a2, g2, k_k, k_a, r_k, tm):
    B, T, D = x.shape
    C = MIX_WIDTH
    n_in = w_in.shape[1]
    zeros = jnp.zeros((DECAY_LORA, C), F32)
    w2p = jnp.concatenate([w2, zeros], axis=0).astype(BF16)
    a2p = jnp.concatenate([zeros, a2], axis=0).astype(BF16)
    gsum = _group_matrix(GROUP_COLS, HEAD_DIM, 1.0)
    t_idx = jnp.arange(GROUP_COLS)
    tri = ((t_idx[:, None] // CHUNK == t_idx[None, :] // CHUNK)
           & (t_idx[None, :] <= t_idx[:, None])).astype(BF16)
    row = lambda p: p.reshape(1, -1)
    big = jax.ShapeDtypeStruct((B, HEAD_PAIRS, T, LANES), F32)
    return pl.pallas_call(
        _rwkv_in_kernel,
        grid=(B, T // tm),
        in_specs=[pl.BlockSpec((1, tm, D), lambda b, t: (b, t, 0)),
                  _const_spec((1, D)), _const_spec((D, n_in)), _const_spec((1, RWKV_COLS)),
                  _const_spec((1, C)), _const_spec((2 * DECAY_LORA, C)),
                  _const_spec((1, C)), _const_spec((2 * ICLR_LORA, C)),
                  _const_spec((GATE_LORA, C)),
                  _const_spec((1, C)), _const_spec((1, C)), _const_spec((1, C)),
                  _const_spec((GROUP_COLS, GROUP_COLS)), _const_spec((GROUP_COLS, GROUP_COLS))],
        out_specs=[_slab_spec(tm)] * 8 + [pl.BlockSpec((1, tm, MEM_WIDTH), lambda b, t: (b, t, 0))],
        out_shape=[big] * 8 + [jax.ShapeDtypeStruct((B, T, MEM_WIDTH), F32)],
        scratch_shapes=[pltpu.VMEM((8, RWKV_COLS), F32)],
        compiler_params=_params("arbitrary", "arbitrary"),
        name="rwkv_in",
    )(x, row(norm_g), w_in.astype(BF16), row(mu), row(w0), w2p, row(a0), a2p, g2.astype(BF16),
      row(k_k), row(k_a), row(r_k), gsum, tri)


def _rwkv_scan_kernel(r_ref, k_ref, v_ref, kn_ref, a_ref, cs_ref, bonus_ref, g_ref,
                      lng_ref, lnb_ref, gmean_ref, out_ref,
                      s_ref, rhat_ref, y_ref, m_ref, c_ref):
    L = CHUNK
    tc = r_ref.shape[2]
    n_chunks = tc // L

    @pl.when(pl.program_id(1) == 0)
    def _():
        s_ref[...] = jnp.zeros_like(s_ref)

    lane = lax.broadcasted_iota(jnp.int32, (L, LANES), 1)
    row = lax.broadcasted_iota(jnp.int32, (L, LANES), 0)
    head0 = lane < HEAD_DIM
    src = lane & (HEAD_DIM - 1)
    strict = row > src
    incl = row >= src
    r2 = lax.broadcasted_iota(jnp.int32, (LANES, LANES), 0)
    c2 = lax.broadcasted_iota(jnp.int32, (LANES, LANES), 1)
    same_head = (r2 >> 6) == (c2 >> 6)
    eye = r2 == c2
    eye_pk = jnp.where(row == src, 1.0, 0.0)
    first_row = (lax.broadcasted_iota(jnp.int32, (tc, LANES), 0) & (L - 1)) == 0

    def stack2(t):
        return jnp.concatenate([jnp.where(head0, t, 0.0), jnp.where(head0, 0.0, t)], axis=0)

    def side2(x, y):
        return jnp.concatenate([stack2(x), stack2(y)], axis=1).astype(BF16)

    def phase_a(hps):
        cut = lambda t: [t[c * L:(c + 1) * L] for c in range(n_chunks)]
        rt, kt, at, bt, vv, w_last = [], [], [], [], [], []
        for hp in hps:
            kn = kn_ref[0, hp]
            cs = cs_ref[0, hp]
            cs_prev = jnp.where(first_row, 0.0, pltpu.roll(cs, 1, axis=0))
            w_t = jnp.exp(cs)
            w_inv = jnp.exp(-cs)
            rt += cut(r_ref[0, hp] * w_t)
            kt += cut(k_ref[0, hp] * w_inv)
            at += cut(-kn * jnp.exp(cs_prev))
            bt += cut(kn * a_ref[0, hp] * w_inv)
            vv += cut(v_ref[0, hp])
            w_last += [w_t[(c + 1) * L - 1:(c + 1) * L] for c in range(n_chunks)]
        chunks = range(len(hps) * n_chunks)

        sc = [_dot_nt(jnp.concatenate([at[c], rt[c]], axis=0).astype(BF16),
                      jnp.concatenate([stack2(bt[c]), stack2(kt[c])], axis=0).astype(BF16))
              for c in chunks]
        a_ab = [jnp.where(strict, s[:L, :LANES], 0.0) for s in sc]
        a_ak = [jnp.where(strict, s[:L, LANES:], 0.0).astype(BF16) for s in sc]
        a_rb = [jnp.where(incl, s[L:, :LANES], 0.0).astype(BF16) for s in sc]
        a_rk = [jnp.where(incl, s[L:, LANES:], 0.0).astype(BF16) for s in sc]

        npow = a_ab
        inv = [eye_pk + n for n in npow]
        nbd = [stack2(n).astype(BF16) for n in npow]
        npow = [_dot(n.astype(BF16), b) for n, b in zip(npow, nbd)]
        for _ in range(int(math.log2(L)) - 2):
            nbd = [stack2(n).astype(BF16) for n in npow]
            both = [_dot(jnp.concatenate([n, i], axis=0).astype(BF16), b)
                    for n, i, b in zip(npow, inv, nbd)]
            npow = [t[:L] for t in both]
            inv = [i + t[L:] for i, t in zip(inv, both)]
        nbd = [stack2(n).astype(BF16) for n in npow]
        t_pk = [(i + _dot(i.astype(BF16), b)).astype(BF16) for i, b in zip(inv, nbd)]

        v2 = [stack2(t).astype(BF16) for t in vv]
        av = [_dot(jnp.concatenate([a_ak[c], a_rk[c]], axis=0), v2[c]) for c in chunks]
        q0 = [t[:L] for t in av]
        arkv = [t[L:] for t in av]
        pq = [_dot(t_pk[c], side2(at[c], q0[c])) for c in chunks]
        ar = [_dot(a_rb[c], side2(pq[c][:, :LANES], pq[c][:, LANES:])) for c in chunks]
        mc = [_dot(jnp.concatenate([bt[c] * w_last[c], kt[c] * w_last[c]], axis=0).T.astype(BF16),
                   jnp.concatenate(
                       [pq[c], jnp.concatenate([jnp.zeros_like(vv[c]), vv[c]], axis=1)], axis=0).astype(BF16))
              for c in chunks]
        for i in chunks:
            hp, c = hps[i // n_chunks], i % n_chunks
            rows = pl.ds(c * L, L)
            rhat_ref[hp, rows, :] = (rt[i] + ar[i][:, :LANES]).astype(BF16)
            y_ref[hp, rows, :] = ar[i][:, LANES:] + arkv[i]
            m_ref[hp, c] = (jnp.where(same_head, mc[i][:, :LANES], 0.0)
                            + jnp.where(eye, w_last[i], 0.0)).astype(BF16)
            c_ref[hp, c] = jnp.where(same_head, mc[i][:, LANES:], 0.0)

    def phase_b_step(c, hps):
        rows = pl.ds(c * L, L)
        sb = [s_ref[hp].astype(BF16) for hp in hps]
        y = [_dot(rhat_ref[hp, rows, :], b) for hp, b in zip(hps, sb)]
        s_new = [_dot(m_ref[hp, c], b) for hp, b in zip(hps, sb)]
        for i, hp in enumerate(hps):
            y_ref[hp, rows, :] = y_ref[hp, rows, :] + y[i]
            s_ref[hp] = s_new[i] + c_ref[hp, c]

    for g in range(0, HEAD_PAIRS, SCAN_PAIRS):
        phase_a(list(range(g, g + SCAN_PAIRS)))
    for c in range(n_chunks):
        phase_b_step(c, list(range(HEAD_PAIRS)))

    gmean2 = gmean_ref[...]

    def head_mean(t):
        hi = t.astype(BF16)
        lo = (t - hi.astype(F32)).astype(BF16)
        return _dot(jnp.concatenate([hi, lo], axis=1), gmean2)

    pairs = range(HEAD_PAIRS)
    ys = [y_ref[hp] for hp in pairs]
    ds = [y - head_mean(y) for y in ys]
    var = [head_mean(d * d) for d in ds]
    for hp in pairs:
        y = ds[hp] * lax.rsqrt(var[hp] + GN_EPS) * lng_ref[hp] + lnb_ref[hp]
        out_ref[0, hp] = ((y + bonus_ref[0, hp]) * g_ref[0, hp]).astype(BF16)


def _rwkv_scan(r, k, v, kn, a, cs, bonus, g, lnx_g, lnx_b, tc):
    B, HP, T, _ = r.shape
    gmean = _group_matrix(LANES, HEAD_DIM, 1.0 / HEAD_DIM)
    gmean = jnp.concatenate([gmean, gmean], axis=0)
    n_chunks = tc // CHUNK
    return pl.pallas_call(
        _rwkv_scan_kernel,
        grid=(B, T // tc),
        in_specs=[_slab_spec(tc)] * 8 + [_const_spec((HP, 1, LANES))] * 2 + [_const_spec((2 * LANES, LANES))],
        out_specs=_slab_spec(tc),
        out_shape=jax.ShapeDtypeStruct((B, HP, T, LANES), BF16),
        scratch_shapes=[pltpu.VMEM((HP, LANES, LANES), F32),
                        pltpu.VMEM((HP, tc, LANES), BF16),
                        pltpu.VMEM((HP, tc, LANES), F32),
                        pltpu.VMEM((HP, n_chunks, LANES, LANES), BF16),
                        pltpu.VMEM((HP, n_chunks, LANES, LANES), F32)],
        compiler_params=_params("arbitrary", "arbitrary"),
        name="rwkv_scan",
    )(r, k, v, kn, a, cs, bonus, g, lnx_g.reshape(HP, 1, LANES), lnx_b.reshape(HP, 1, LANES), gmean)


def _diff_in_kernel(x_ref, ng_ref, w_ref, qg_ref, kg_ref, gmean_ref, q_out, k_out, v_out, qm_out):
    C = MIX_WIDTH
    h = _rms_rows(x_ref[0], ng_ref[...]).astype(BF16)
    proj = _dot(h, w_ref[...])
    gmean = gmean_ref[...]

    def head_norm(t):
        ms = jnp.concatenate(
            [_dot(jnp.square(t[:, i:i + GROUP_COLS]).astype(BF16), gmean)
             for i in range(0, C, GROUP_COLS)], axis=1)
        return t * lax.rsqrt(ms + RMS_EPS)

    q_out[0] = (head_norm(proj[:, :C]) * qg_ref[...]).astype(BF16)
    k_out[0] = (head_norm(proj[:, C:2 * C]) * kg_ref[...]).astype(BF16)
    v_out[0] = proj[:, 2 * C:3 * C].T.astype(BF16)
    qm_out[0] = proj[:, 3 * C:]


def _diff_in(x, norm_g, w_in, q_g, k_g, tm):
    B, T, D = x.shape
    C = MIX_WIDTH
    n_in = w_in.shape[1]
    gmean = _group_matrix(GROUP_COLS, HEAD_DIM, 1.0 / HEAD_DIM)
    qg = jnp.tile(q_g.reshape(-1), HEAD_PAIRS).reshape(1, C) * (HEAD_DIM ** -0.5 * LOG2_E)
    kg = jnp.tile(k_g.reshape(-1), HEAD_PAIRS).reshape(1, C)
    tile = pl.BlockSpec((1, tm, C), lambda b, t: (b, t, 0))
    big = jax.ShapeDtypeStruct((B, T, C), BF16)
    return pl.pallas_call(
        _diff_in_kernel,
        grid=(B, T // tm),
        in_specs=[pl.BlockSpec((1, tm, D), lambda b, t: (b, t, 0)),
                  _const_spec((1, D)), _const_spec((D, n_in)),
                  _const_spec((1, C)), _const_spec((1, C)), _const_spec((GROUP_COLS, GROUP_COLS))],
        out_specs=[tile, tile, pl.BlockSpec((1, C, tm), lambda b, t: (b, 0, t)),
                   pl.BlockSpec((1, tm, MEM_WIDTH), lambda b, t: (b, t, 0))],
        out_shape=[big, big, jax.ShapeDtypeStruct((B, C, T), BF16),
                   jax.ShapeDtypeStruct((B, T, MEM_WIDTH), F32)],
        compiler_params=_params("arbitrary", "arbitrary"),
        name="diff_in",
    )(x, norm_g.reshape(1, D), w_in.astype(BF16), qg, kg, gmean)


def _diff_flash_kernel(q_ref, k_ref, vt_ref, qfeat_ref, kfeat_ref,
                       lq1_ref, lk1_ref, lq2_ref, lk2_ref, sg_ref,
                       out_ref, sa_ref, sb_ref, *, lambda_init, tq):
    seq = q_ref.shape[1]
    lam = (jnp.exp(jnp.sum(lq1_ref[...] * lk1_ref[...], axis=-1, keepdims=True))
           - jnp.exp(jnp.sum(lq2_ref[...] * lk2_ref[...], axis=-1, keepdims=True)) + lambda_init)
    dim = lax.broadcasted_iota(jnp.int32, (LANES, tq), 0)
    key_loc = lax.broadcasted_iota(jnp.int32, (tq, 1), 0)
    q_loc = lax.broadcasted_iota(jnp.int32, (1, 2 * tq), 1) & (tq - 1)
    qfeat = jnp.concatenate([qfeat_ref[0]] * (2 * tq // LANES), axis=1)
    ones_rows = jnp.ones((ONES_ROWS, tq), BF16)

    def stacked_queries(t):
        qt = q_ref[0, t * tq:(t + 1) * tq, :].astype(F32).T
        zero = jnp.zeros_like(qt)
        top = jnp.concatenate(
            [jnp.where(dim < HEAD_DIM, qt, zero), jnp.where(dim < HEAD_DIM, zero, qt)], axis=1)
        return jnp.concatenate([top.astype(BF16), qfeat], axis=0)

    def scores(qs, h):
        rows = slice(h * tq, (h + 1) * tq)
        return _dot(jnp.concatenate([k_ref[0, rows, :], kfeat_ref[rows, :]], axis=1), qs)

    def softmax_pv(s, h, carry, masked):
        m, acc = carry
        if masked:
            s = jnp.where(key_loc <= q_loc, s, NEG_BIG)
        m_new = jnp.maximum(m, jnp.max(s, axis=0, keepdims=True))
        p = jnp.exp2(s - m_new).astype(BF16)
        vt = vt_ref[0, :, h * tq:(h + 1) * tq]
        pv = _dot(jnp.concatenate([vt, ones_rows], axis=0), p)
        return m_new, jnp.exp2(m - m_new) * acc + pv

    tasks = [(t, h) for t in range(seq // tq) for h in range(t + 1)]
    bufs = (sa_ref, sb_ref)
    fresh = (jnp.full((1, 2 * tq), NEG_BIG, F32), jnp.zeros((LANES + ONES_ROWS, 2 * tq), F32))
    qs = stacked_queries(0)
    bufs[0][...] = scores(qs, 0)
    carry = fresh
    for n, (t, h) in enumerate(tasks):
        if n + 1 < len(tasks):
            t_next, h_next = tasks[n + 1]
            qs_next = qs if t_next == t else stacked_queries(t_next)
            bufs[(n + 1) % 2][...] = scores(qs_next, h_next)
        carry = softmax_pv(bufs[n % 2][...], h, carry, h == t)
        if h == t:
            acc = carry[1]
            o = acc[:LANES] / acc[LANES:LANES + 1]
            o = o[:, :tq] - lam * o[:, tq:]
            o = o * lax.rsqrt(jnp.mean(o * o, axis=0, keepdims=True) + RMS_EPS) * sg_ref[...]
            out_ref[0, 0, t * tq:(t + 1) * tq, :] = (o * (1.0 - lambda_init)).T.astype(BF16)
            carry = fresh
        if n + 1 < len(tasks):
            qs = qs_next


def _diff_flash(q, k, vt, lq1, lk1, lq2, lk2, subln_g, lambda_init, tq):
    B, T, C = q.shape
    H = C // LANES
    slopes = jnp.exp2(-8.0 * jnp.arange(1, H + 1, dtype=F32) / H) * LOG2_E
    assert T // FEAT_RADIX <= 256
    s1 = slopes.astype(BF16)
    r1 = slopes - s1.astype(F32)
    s2 = r1.astype(BF16)
    s3 = (r1 - s2.astype(F32)).astype(BF16)
    pieces = jnp.stack([s1, s2, s3], axis=1).astype(F32)
    qfeat = jnp.concatenate([pieces * FEAT_RADIX, pieces, jnp.zeros((H, LANES - 6), F32)], axis=1)
    qfeat = jnp.broadcast_to(qfeat.astype(BF16)[:, :, None], (H, LANES, LANES))
    pos = jnp.arange(T)
    hi = jnp.broadcast_to((pos // FEAT_RADIX)[:, None], (T, 3))
    lo = jnp.broadcast_to((pos % FEAT_RADIX)[:, None], (T, 3))
    kfeat = jnp.concatenate([hi, lo, jnp.zeros((T, LANES - 6), jnp.int32)], axis=1).astype(BF16)
    lvec = _const_spec((1, HEAD_DIM))
    seq = pl.BlockSpec((1, T, LANES), lambda b, h: (b, 0, h))
    return pl.pallas_call(
        functools.partial(_diff_flash_kernel, lambda_init=lambda_init, tq=tq),
        grid=(B, H),
        in_specs=[seq, seq, pl.BlockSpec((1, LANES, T), lambda b, h: (b, h, 0)),
                  pl.BlockSpec((1, LANES, LANES), lambda b, h: (h, 0, 0)), _const_spec((T, LANES)),
                  lvec, lvec, lvec, lvec, _const_spec((LANES, 1))],
        out_specs=pl.BlockSpec((1, 1, T, LANES), lambda b, h: (b, h, 0, 0)),
        out_shape=jax.ShapeDtypeStruct((B, H, T, LANES), BF16),
        scratch_shapes=[pltpu.VMEM((tq, 2 * tq), F32)] * 2,
        compiler_params=_params("arbitrary", "arbitrary"),
        name="diff_flash",
    )(q, k, vt, qfeat, kfeat,
      lq1.reshape(1, -1), lk1.reshape(1, -1), lq2.reshape(1, -1), lk2.reshape(1, -1),
      subln_g.reshape(LANES, 1))


def _attn_out_kernel(mix_ref, qm_ref, km_ref, vm_ref, x_ref, qg_ref, kg_ref, w_ref, gmean_ref, out_ref):
    qm = qm_ref[0]
    ms = _dot((qm * qm).astype(BF16), gmean_ref[...])
    qn = qm * lax.rsqrt(ms + RMS_EPS) * qg_ref[...]
    kb = (km_ref[0] * kg_ref[...]).astype(BF16)
    vb = vm_ref[0].astype(BF16)
    lane = lax.broadcasted_iota(jnp.int32, qn.shape, 1)
    mem = jnp.zeros_like(qn)
    for hd in range(MEM_HEADS):
        in_head = (lane >= hd * HEAD_DIM) & (lane < (hd + 1) * HEAD_DIM)
        s = _dot_nt(jnp.where(in_head, qn, 0.0).astype(BF16), kb)
        p = jnp.exp(s - jnp.max(s, axis=-1, keepdims=True))
        o = _dot(p.astype(BF16), vb) / jnp.sum(p, axis=-1, keepdims=True)
        mem = jnp.where(in_head, o, mem)
    both = jnp.concatenate([mix_ref[0, hp] for hp in range(HEAD_PAIRS)] + [mem.astype(BF16)], axis=1)
    out_ref[0] = x_ref[0] + _dot(both, w_ref[...])


def _attn_out(mix, qm, k_mem, v_mem, x, q_g, k_g, w_out, tm):
    B, T, D = x.shape
    M = k_mem.shape[1]
    gmean = _group_matrix(MEM_WIDTH, HEAD_DIM, 1.0 / HEAD_DIM)
    qg = jnp.tile(q_g, MEM_HEADS).reshape(1, MEM_WIDTH) * HEAD_DIM ** -0.5
    kg = jnp.tile(k_g, MEM_HEADS).reshape(1, MEM_WIDTH)
    rows = lambda n: pl.BlockSpec((1, tm, n), lambda b, t: (b, t, 0))
    memspec = pl.BlockSpec((1, M, MEM_WIDTH), lambda b, t: (b, 0, 0))
    return pl.pallas_call(
        _attn_out_kernel,
        grid=(B, T // tm),
        in_specs=[_slab_spec(tm), rows(MEM_WIDTH), memspec, memspec, rows(D),
                  _const_spec((1, MEM_WIDTH)), _const_spec((1, MEM_WIDTH)),
                  _const_spec((D, D)),
                  _const_spec((MEM_WIDTH, MEM_WIDTH))],
        out_specs=rows(D),
        out_shape=jax.ShapeDtypeStruct((B, T, D), F32),
        compiler_params=_params("arbitrary", "arbitrary"),
        name="attn_out",
    )(mix, qm, k_mem, v_mem, x, qg, kg, w_out.astype(BF16), gmean)


def _ffn_kernel(x_ref, ng_ref, w1_ref, w2_ref, out_ref, *, ff_chunk):
    x = x_ref[0]
    h = _rms_rows(x, ng_ref[...]).astype(BF16)
    acc = x
    for c in range(0, w1_ref.shape[1], ff_chunk):
        u = jnp.maximum(_dot(h, w1_ref[:, c:c + ff_chunk]), 0.0)
        acc = acc + _dot((u * u).astype(BF16), w2_ref[c:c + ff_chunk, :])
    out_ref[0] = acc


def _ffn(x, norm_g, w1, w2, tm):
    B, T, D = x.shape
    F = w1.shape[1]
    rows = pl.BlockSpec((1, tm, D), lambda b, t: (b, t, 0))
    return pl.pallas_call(
        functools.partial(_ffn_kernel, ff_chunk=D),
        grid=(B, T // tm),
        in_specs=[rows, _const_spec((1, D)), _const_spec((D, F)), _const_spec((F, D))],
        out_specs=rows,
        out_shape=jax.ShapeDtypeStruct((B, T, D), F32),
        compiler_params=_params("arbitrary", "arbitrary"),
        name="ffn",
    )(x, norm_g.reshape(1, D), w1.astype(BF16), w2.astype(BF16))


def _diff_lambda_init(layer):
    return 0.8 - 0.6 * math.exp(-0.3 * layer)


def _row_tiles(seq):
    tiles = {"rwkv_in": 512, "rwkv_scan": 512, "diff_in": 1024, "diff_flash": 512, "attn_out": 1024, "ffn": 1024}
    tiles = {name: min(seq, rows) for name, rows in tiles.items()}
    assert all(seq % rows == 0 for rows in tiles.values())
    assert tiles["rwkv_in"] % GROUP_COLS == 0 and tiles["rwkv_scan"] % CHUNK == 0
    return tiles


def kernel(x, mem, norm_mix_g, norm_ffn_g, w_out, w_ff1, w_ff2, mem_norm_g, w_mem_kv, mem_q_norm_g, mem_k_norm_g, rw_in, rw_mu, rw_w0, rw_w2, rw_a0, rw_a2, rw_g2, rw_k_k, rw_k_a, rw_r_k, rw_lnx_g, rw_lnx_b, df_in, df_q_norm_g, df_k_norm_g, df_lq1, df_lk1, df_lq2, df_lk2, df_subln_g):
    tiles = _row_tiles(x.shape[1])
    depth = norm_mix_g.shape[0]
    k_mem, v_mem = _mem_kv(mem, mem_norm_g, w_mem_kv)
    for layer in range(depth):
        j = layer // 2
        if layer % 2 == 0:
            r, k, v, kn, a, cs, bonus, g, qm = _rwkv_in(
                x, norm_mix_g[layer], rw_in[j], rw_mu[j], rw_w0[j], rw_w2[j], rw_a0[j], rw_a2[j],
                rw_g2[j], rw_k_k[j], rw_k_a[j], rw_r_k[j], tiles["rwkv_in"])
            mix = _rwkv_scan(r, k, v, kn, a, cs, bonus, g, rw_lnx_g[j], rw_lnx_b[j], tiles["rwkv_scan"])
        else:
            q, k, v, qm = _diff_in(x, norm_mix_g[layer], df_in[j], df_q_norm_g[j], df_k_norm_g[j],
                                   tiles["diff_in"])
            mix = _diff_flash(q, k, v, df_lq1[j], df_lk1[j], df_lq2[j], df_lk2[j], df_subln_g[j],
                              _diff_lambda_init(layer), tiles["diff_flash"])
        x = _attn_out(mix, qm, k_mem, v_mem, x, mem_q_norm_g[layer], mem_k_norm_g[layer], w_out[layer],
                      tiles["attn_out"])
        x = _ffn(x, norm_ffn_g[layer], w_ff1[layer], w_ff2[layer], tiles["ffn"])
    return x
```

```python
import functools
import math

import jax
import jax.numpy as jnp
from jax import lax
from jax.experimental import pallas as pl
from jax.experimental.pallas import tpu as pltpu

F32 = jnp.float32
BF16 = jnp.bfloat16

HEAD_DIM = 64
MIX_WIDTH = 768
MEM_WIDTH = 256
MEM_HEADS = 4
DECAY_LORA = 64
ICLR_LORA = 64
GATE_LORA = 128
RWKV_COLS = 3 * MIX_WIDTH + DECAY_LORA + ICLR_LORA + GATE_LORA
RMS_EPS = 1e-6
GN_EPS = 64e-5

LANES = 128
HEAD_PAIRS = MIX_WIDTH // LANES
GROUP_COLS = 256
CHUNK = 64
SCAN_PAIRS = 3
NEG_BIG = -1e30
LOG2_E = math.log2(math.e)
FEAT_RADIX = 32
ONES_ROWS = 16
VMEM_LIMIT = 56 * 1024 * 1024


def _dot(a, b):
    return jnp.dot(a, b, preferred_element_type=F32)


def _dot_nt(a, b):
    return lax.dot_general(a, b, (((1,), (1,)), ((), ())), preferred_element_type=F32)


def _dot_split(x, m):
    hi = x.astype(BF16)
    lo = (x - hi.astype(F32)).astype(BF16)
    return _dot(hi, m) + _dot(lo, m)


def _dot_split_left(m, x):
    hi = x.astype(BF16)
    lo = (x - hi.astype(F32)).astype(BF16)
    return _dot(m, hi) + _dot(m, lo)


def _sigmoid(x):
    return 1.0 / (1.0 + jnp.exp(-x))


def _rms_rows(x, g):
    return x * lax.rsqrt(jnp.mean(x * x, axis=-1, keepdims=True) + RMS_EPS) * g


def _group_matrix(n, group, value):
    i = jnp.arange(n) // group
    return jnp.where(i[:, None] == i[None, :], value, 0.0).astype(BF16)


def _params(*sem):
    return pltpu.CompilerParams(dimension_semantics=sem, vmem_limit_bytes=VMEM_LIMIT)


def _const_spec(shape):
    nd = len(shape)
    return pl.BlockSpec(shape, lambda *_: (0,) * nd)


def _slab_spec(tm):
    return pl.BlockSpec((1, HEAD_PAIRS, tm, LANES), lambda b, t: (b, 0, t, 0))


def _mem_kv_kernel(mem_ref, g_ref, w_ref, gm_ref, k_out, v_out):
    h = _rms_rows(mem_ref[0], g_ref[...])
    kv = _dot(h.astype(BF16), w_ref[...])
    k = kv[:, :MEM_WIDTH]
    ms = _dot_split(k * k, gm_ref[...])
    k_out[0] = k * lax.rsqrt(ms + RMS_EPS)
    v_out[0] = kv[:, MEM_WIDTH:]


def _mem_kv(mem, mem_norm_g, w_mem_kv):
    B, M, D = mem.shape
    gm = _group_matrix(MEM_WIDTH, HEAD_DIM, 1.0 / HEAD_DIM)
    out = jax.ShapeDtypeStruct((B, M, MEM_WIDTH), F32)
    return pl.pallas_call(
        _mem_kv_kernel,
        grid=(B,),
        in_specs=[pl.BlockSpec((1, M, D), lambda b: (b, 0, 0)),
                  _const_spec((1, D)), _const_spec((D, 2 * MEM_WIDTH)),
                  _const_spec((MEM_WIDTH, MEM_WIDTH))],
        out_specs=[pl.BlockSpec((1, M, MEM_WIDTH), lambda b: (b, 0, 0))] * 2,
        out_shape=[out, out],
        compiler_params=_params("arbitrary"),
        name="mem_kv",
    )(mem, mem_norm_g.reshape(1, D), w_mem_kv.astype(BF16), gm)


def _rwkv_in_kernel(x_ref, ng_ref, w_ref, mu_ref, w0_ref, w2_ref, a0_ref, a2_ref, g2_ref,
                    kk_ref, ka_ref, rk_ref, gsum_ref, tri_ref,
                    r_out, k_out, v_out, kn_out, a_out, cs_out, bonus_out, g_out, qm_out,
                    carry_ref):
    C = MIX_WIDTH
    tm = x_ref.shape[1]
    ts = GROUP_COLS

    @pl.when(pl.program_id(1) == 0)
    def _():
        carry_ref[...] = jnp.zeros_like(carry_ref)

    gsum = gsum_ref[...]

    def head_sum(t):
        return jnp.concatenate(
            [_dot(t[:, i:i + GROUP_COLS].astype(BF16), gsum) for i in range(0, C, GROUP_COLS)], axis=1)

    def project(r0):
        h = _rms_rows(x_ref[0, r0:r0 + ts, :], ng_ref[...]).astype(BF16)
        return _dot(h, w_ref[...])

    def finish(proj, prev_row, r0):
        cur = proj[:, :RWKV_COLS]
        row = lax.broadcasted_iota(jnp.int32, (ts, 1), 0)
        prev = jnp.where(row == 0, prev_row, pltpu.roll(cur, 1, axis=0))
        slab = cur + (prev - cur) * mu_ref[...]

        r = slab[:, 0:C]
        k = slab[:, C:2 * C]
        v = slab[:, 2 * C:3 * C]
        wa = slab[:, 3 * C:3 * C + DECAY_LORA + ICLR_LORA]
        gd = slab[:, 3 * C + DECAY_LORA + ICLR_LORA:RWKV_COLS]

        d = w0_ref[...] + _dot(jnp.tanh(wa).astype(BF16), w2_ref[...])
        log_decay = -math.exp(-0.5) * _sigmoid(d)
        a = _sigmoid(a0_ref[...] + _dot(wa.astype(BF16), a2_ref[...]))
        g = _dot(_sigmoid(gd).astype(BF16), g2_ref[...])

        kn = k * kk_ref[...]
        kn = kn * lax.rsqrt(jnp.maximum(head_sum(kn * kn), 1e-24))
        k2 = k * (1.0 + (a - 1.0) * ka_ref[...])
        bonus = head_sum(r * k2 * rk_ref[...]) * v
        cs = _dot_split_left(tri_ref[...], log_decay)
        outs = ((r_out, r), (k_out, k2), (v_out, v), (kn_out, kn), (a_out, a), (cs_out, cs),
                (bonus_out, bonus), (g_out, g))
        for ref, val in outs:
            for hp in range(HEAD_PAIRS):
                ref[0, hp, r0:r0 + ts, :] = val[:, hp * LANES:(hp + 1) * LANES]
        qm_out[0, r0:r0 + ts, :] = proj[:, RWKV_COLS:].astype(BF16)
        return cur[ts - 1:ts, :]

    prev_row = carry_ref[0:1, :]
    proj = project(0)
    for r0 in range(0, tm, ts):
        proj_next = project(r0 + ts) if r0 + ts < tm else None
        prev_row = finish(proj, prev_row, r0)
        proj = proj_next
    carry_ref[0:1, :] = prev_row


def _rwkv_in(x, norm_g, w_in, mu, w0, w2, a0, a2, g2, k_k, k_a, r_k, tm):
    B, T, D = x.shape
    C = MIX_WIDTH
    n_in = w_in.shape[1]
    zeros = jnp.zeros((DECAY_LORA, C), F32)
    w2p = jnp.concatenate([w2, zeros], axis=0).astype(BF16)
    a2p = jnp.concatenate([zeros, a2], axis=0).astype(BF16)
    gsum = _group_matrix(GROUP_COLS, HEAD_DIM, 1.0)
    t_idx = jnp.arange(GROUP_COLS)
    tri = ((t_idx[:, None] // CHUNK == t_idx[None, :] // CHUNK)
           & (t_idx[None, :] <= t_idx[:, None])).astype(BF16)
    row = lambda p: p.reshape(1, -1)
    big = jax.ShapeDtypeStruct((B, HEAD_PAIRS, T, LANES), F32)
    return pl.pallas_call(
        _rwkv_in_kernel,
        grid=(B, T // tm),
        in_specs=[pl.BlockSpec((1, tm, D), lambda b, t: (b, t, 0)),
                  _const_spec((1, D)), _const_spec((D, n_in)), _const_spec((1, RWKV_COLS)),
                  _const_spec((1, C)), _const_spec((2 * DECAY_LORA, C)),
                  _const_spec((1, C)), _const_spec((2 * ICLR_LORA, C)),
                  _const_spec((GATE_LORA, C)),
                  _const_spec((1, C)), _const_spec((1, C)), _const_spec((1, C)),
                  _const_spec((GROUP_COLS, GROUP_COLS)), _const_spec((GROUP_COLS, GROUP_COLS))],
        out_specs=[_slab_spec(tm)] * 8 + [pl.BlockSpec((1, tm, MEM_WIDTH), lambda b, t: (b, t, 0))],
        out_shape=[big] * 8 + [jax.ShapeDtypeStruct((B, T, MEM_WIDTH), BF16)],
        scratch_shapes=[pltpu.VMEM((8, RWKV_COLS), F32)],
        compiler_params=_params("arbitrary", "arbitrary"),
        name="rwkv_in",
    )(x, row(norm_g), w_in.astype(BF16), row(mu), row(w0), w2p, row(a0), a2p, g2.astype(BF16),
      row(k_k), row(k_a), row(r_k), gsum, tri)


def _rwkv_scan_kernel(r_ref, k_ref, v_ref, kn_ref, a_ref, cs_ref, bonus_ref, g_ref,
                      lng_ref, lnb_ref, gmean_ref, out_ref,
                      s_ref, rhat_ref, y_ref, m_ref, c_ref):
    L = CHUNK
    tc = r_ref.shape[2]
    n_chunks = tc // L

    @pl.when(pl.program_id(1) == 0)
    def _():
        s_ref[...] = jnp.zeros_like(s_ref)

    lane = lax.broadcasted_iota(jnp.int32, (L, LANES), 1)
    row = lax.broadcasted_iota(jnp.int32, (L, LANES), 0)
    head0 = lane < HEAD_DIM
    src = lane & (HEAD_DIM - 1)
    strict = row > src
    incl = row >= src
    r2 = lax.broadcasted_iota(jnp.int32, (LANES, LANES), 0)
    c2 = lax.broadcasted_iota(jnp.int32, (LANES, LANES), 1)
    same_head = (r2 >> 6) == (c2 >> 6)
    eye = r2 == c2
    eye_pk = jnp.where(row == src, 1.0, 0.0)
    first_row = (lax.broadcasted_iota(jnp.int32, (tc, LANES), 0) & (L - 1)) == 0

    def stack2(t):
        return jnp.concatenate([jnp.where(head0, t, 0.0), jnp.where(head0, 0.0, t)], axis=0)

    def side2(x, y):
        return jnp.concatenate([stack2(x), stack2(y)], axis=1).astype(BF16)

    def phase_a(hps):
        cut = lambda t: [t[c * L:(c + 1) * L] for c in range(n_chunks)]
        rt, kt, at, bt, vv, w_last = [], [], [], [], [], []
        for hp in hps:
            kn = kn_ref[0, hp]
            cs = cs_ref[0, hp]
            cs_prev = jnp.where(first_row, 0.0, pltpu.roll(cs, 1, axis=0))
            w_t = jnp.exp(cs)
            w_inv = jnp.exp(-cs)
            rt += cut(r_ref[0, hp] * w_t)
            kt += cut(k_ref[0, hp] * w_inv)
            at += cut(-kn * jnp.exp(cs_prev))
            bt += cut(kn * a_ref[0, hp] * w_inv)
            vv += cut(v_ref[0, hp])
            w_last += [w_t[(c + 1) * L - 1:(c + 1) * L] for c in range(n_chunks)]
        chunks = range(len(hps) * n_chunks)

        sc = [_dot_nt(jnp.concatenate([at[c], rt[c]], axis=0).astype(BF16),
                      jnp.concatenate([stack2(bt[c]), stack2(kt[c])], axis=0).astype(BF16))
              for c in chunks]
        a_ab = [jnp.where(strict, s[:L, :LANES], 0.0) for s in sc]
        a_ak = [jnp.where(strict, s[:L, LANES:], 0.0).astype(BF16) for s in sc]
        a_rb = [jnp.where(incl, s[L:, :LANES], 0.0).astype(BF16) for s in sc]
        a_rk = [jnp.where(incl, s[L:, LANES:], 0.0).astype(BF16) for s in sc]

        npow = a_ab
        inv = [eye_pk + n for n in npow]
        nbd = [stack2(n).astype(BF16) for n in npow]
        npow = [_dot(n.astype(BF16), b) for n, b in zip(npow, nbd)]
        for _ in range(int(math.log2(L)) - 2):
            nbd = [stack2(n).astype(BF16) for n in npow]
            both = [_dot(jnp.concatenate([n, i], axis=0).astype(BF16), b)
                    for n, i, b in zip(npow, inv, nbd)]
            npow = [t[:L] for t in both]
            inv = [i + t[L:] for i, t in zip(inv, both)]
        nbd = [stack2(n).astype(BF16) for n in npow]
        t_pk = [(i + _dot(i.astype(BF16), b)).astype(BF16) for i, b in zip(inv, nbd)]

        v2 = [stack2(t).astype(BF16) for t in vv]
        av = [_dot(jnp.concatenate([a_ak[c], a_rk[c]], axis=0), v2[c]) for c in chunks]
        q0 = [t[:L] for t in av]
        arkv = [t[L:] for t in av]
        pq = [_dot(t_pk[c], side2(at[c], q0[c])) for c in chunks]
        ar = [_dot(a_rb[c], side2(pq[c][:, :LANES], pq[c][:, LANES:])) for c in chunks]
        mc = [_dot(jnp.concatenate([bt[c] * w_last[c], kt[c] * w_last[c]], axis=0).T.astype(BF16),
                   jnp.concatenate(
                       [pq[c], jnp.concatenate([jnp.zeros_like(vv[c]), vv[c]], axis=1)], axis=0).astype(BF16))
              for c in chunks]
        for i in chunks:
            hp, c = hps[i // n_chunks], i % n_chunks
            rows = pl.ds(c * L, L)
            rhat_ref[hp, rows, :] = (rt[i] + ar[i][:, :LANES]).astype(BF16)
            y_ref[hp, rows, :] = ar[i][:, LANES:] + arkv[i]
            m_ref[hp, c] = (jnp.where(same_head, mc[i][:, :LANES], 0.0)
                            + jnp.where(eye, w_last[i], 0.0)).astype(BF16)
            c_ref[hp, c] = jnp.where(same_head, mc[i][:, LANES:], 0.0)

    def phase_b_step(c, hps):
        rows = pl.ds(c * L, L)
        sb = [s_ref[hp].astype(BF16) for hp in hps]
        y = [_dot(rhat_ref[hp, rows, :], b) for hp, b in zip(hps, sb)]
        s_new = [_dot(m_ref[hp, c], b) for hp, b in zip(hps, sb)]
        for i, hp in enumerate(hps):
            y_ref[hp, rows, :] = y_ref[hp, rows, :] + y[i]
            s_ref[hp] = s_new[i] + c_ref[hp, c]

    for g in range(0, HEAD_PAIRS, SCAN_PAIRS):
        phase_a(list(range(g, g + SCAN_PAIRS)))
    for c in range(n_chunks):
        phase_b_step(c, list(range(HEAD_PAIRS)))

    gmean2 = gmean_ref[...]

    def head_mean(t):
        hi = t.astype(BF16)
        lo = (t - hi.astype(F32)).astype(BF16)
        return _dot(jnp.concatenate([hi, lo], axis=1), gmean2)

    pairs = range(HEAD_PAIRS)
    ys = [y_ref[hp] for hp in pairs]
    ds = [y - head_mean(y) for y in ys]
    var = [head_mean(d * d) for d in ds]
    for hp in pairs:
        y = ds[hp] * lax.rsqrt(var[hp] + GN_EPS) * lng_ref[hp] + lnb_ref[hp]
        out_ref[0, hp] = ((y + bonus_ref[0, hp]) * g_ref[0, hp]).astype(BF16)


def _rwkv_scan(r, k, v, kn, a, cs, bonus, g, lnx_g, lnx_b, tc):
    B, HP, T, _ = r.shape
    gmean = _group_matrix(LANES, HEAD_DIM, 1.0 / HEAD_DIM)
    gmean = jnp.concatenate([gmean, gmean], axis=0)
    n_chunks = tc // CHUNK
    return pl.pallas_call(
        _rwkv_scan_kernel,
        grid=(B, T // tc),
        in_specs=[_slab_spec(tc)] * 8 + [_const_spec((HP, 1, LANES))] * 2 + [_const_spec((2 * LANES, LANES))],
        out_specs=_slab_spec(tc),
        out_shape=jax.ShapeDtypeStruct((B, HP, T, LANES), BF16),
        scratch_shapes=[pltpu.VMEM((HP, LANES, LANES), F32),
                        pltpu.VMEM((HP, tc, LANES), BF16),
                        pltpu.VMEM((HP, tc, LANES), F32),
                        pltpu.VMEM((HP, n_chunks, LANES, LANES), BF16),
                        pltpu.VMEM((HP, n_chunks, LANES, LANES), F32)],
        compiler_params=_params("arbitrary", "arbitrary"),
        name="rwkv_scan",
    )(r, k, v, kn, a, cs, bonus, g, lnx_g.reshape(HP, 1, LANES), lnx_b.reshape(HP, 1, LANES), gmean)


def _diff_in_kernel(x_ref, ng_ref, w_ref, qg_ref, kg_ref, gmean_ref, q_out, k_out, v_out, qm_out):
    C = MIX_WIDTH
    h = _rms_rows(x_ref[0], ng_ref[...]).astype(BF16)
    proj = _dot(h, w_ref[...])
    gmean = gmean_ref[...]

    def head_norm(t):
        ms = jnp.concatenate(
            [_dot(jnp.square(t[:, i:i + GROUP_COLS]).astype(BF16), gmean)
             for i in range(0, C, GROUP_COLS)], axis=1)
        return t * lax.rsqrt(ms + RMS_EPS)

    q_out[0] = (head_norm(proj[:, :C]) * qg_ref[...]).astype(BF16)
    k_out[0] = (head_norm(proj[:, C:2 * C]) * kg_ref[...]).astype(BF16)
    v_out[0] = proj[:, 2 * C:3 * C].T.astype(BF16)
    qm_out[0] = proj[:, 3 * C:].astype(BF16)


def _diff_in(x, norm_g, w_in, q_g, k_g, tm):
    B, T, D = x.shape
    C = MIX_WIDTH
    n_in = w_in.shape[1]
    gmean = _group_matrix(GROUP_COLS, HEAD_DIM, 1.0 / HEAD_DIM)
    qg = jnp.tile(q_g.reshape(-1), HEAD_PAIRS).reshape(1, C) * (HEAD_DIM ** -0.5 * LOG2_E)
    kg = jnp.tile(k_g.reshape(-1), HEAD_PAIRS).reshape(1, C)
    tile = pl.BlockSpec((1, tm, C), lambda b, t: (b, t, 0))
    big = jax.ShapeDtypeStruct((B, T, C), BF16)
    return pl.pallas_call(
        _diff_in_kernel,
        grid=(B, T // tm),
        in_specs=[pl.BlockSpec((1, tm, D), lambda b, t: (b, t, 0)),
                  _const_spec((1, D)), _const_spec((D, n_in)),
                  _const_spec((1, C)), _const_spec((1, C)), _const_spec((GROUP_COLS, GROUP_COLS))],
        out_specs=[tile, tile, pl.BlockSpec((1, C, tm), lambda b, t: (b, 0, t)),
                   pl.BlockSpec((1, tm, MEM_WIDTH), lambda b, t: (b, t, 0))],
        out_shape=[big, big, jax.ShapeDtypeStruct((B, C, T), BF16),
                   jax.ShapeDtypeStruct((B, T, MEM_WIDTH), BF16)],
        compiler_params=_params("arbitrary", "arbitrary"),
        name="diff_in",
    )(x, norm_g.reshape(1, D), w_in.astype(BF16), qg, kg, gmean)


def _diff_flash_kernel(q_ref, k_ref, vt_ref, qfeat_ref, kfeat_ref,
                       lq1_ref, lk1_ref, lq2_ref, lk2_ref, sg_ref,
                       out_ref, sa_ref, sb_ref, *, lambda_init, tq):
    seq = q_ref.shape[1]
    lam = (jnp.exp(jnp.sum(lq1_ref[...] * lk1_ref[...], axis=-1, keepdims=True))
           - jnp.exp(jnp.sum(lq2_ref[...] * lk2_ref[...], axis=-1, keepdims=True)) + lambda_init)
    dim = lax.broadcasted_iota(jnp.int32, (LANES, tq), 0)
    key_loc = lax.broadcasted_iota(jnp.int32, (tq, 1), 0)
    q_loc = lax.broadcasted_iota(jnp.int32, (1, 2 * tq), 1) & (tq - 1)
    qfeat = jnp.concatenate([qfeat_ref[0]] * (2 * tq // LANES), axis=1)
    ones_rows = jnp.ones((ONES_ROWS, tq), BF16)

    def stacked_queries(t):
        qt = q_ref[0, t * tq:(t + 1) * tq, :].astype(F32).T
        zero = jnp.zeros_like(qt)
        top = jnp.concatenate(
            [jnp.where(dim < HEAD_DIM, qt, zero), jnp.where(dim < HEAD_DIM, zero, qt)], axis=1)
        return jnp.concatenate([top.astype(BF16), qfeat], axis=0)

    def scores(qs, h):
        rows = slice(h * tq, (h + 1) * tq)
        return _dot(jnp.concatenate([k_ref[0, rows, :], kfeat_ref[rows, :]], axis=1), qs)

    def softmax_pv(s, h, carry, masked):
        m, acc = carry
        if masked:
            s = jnp.where(key_loc <= q_loc, s, NEG_BIG)
        m_new = jnp.maximum(m, jnp.max(s, axis=0, keepdims=True))
        p = jnp.exp2(s - m_new).astype(BF16)
        vt = vt_ref[0, :, h * tq:(h + 1) * tq]
        pv = _dot(jnp.concatenate([vt, ones_rows], axis=0), p)
        return m_new, jnp.exp2(m - m_new) * acc + pv

    tasks = [(t, h) for t in range(seq // tq) for h in range(t + 1)]
    bufs = (sa_ref, sb_ref)
    fresh = (jnp.full((1, 2 * tq), NEG_BIG, F32), jnp.zeros((LANES + ONES_ROWS, 2 * tq), F32))
    qs = stacked_queries(0)
    bufs[0][...] = scores(qs, 0)
    carry = fresh
    for n, (t, h) in enumerate(tasks):
        if n + 1 < len(tasks):
            t_next, h_next = tasks[n + 1]
            qs_next = qs if t_next == t else stacked_queries(t_next)
            bufs[(n + 1) % 2][...] = scores(qs_next, h_next)
        carry = softmax_pv(bufs[n % 2][...], h, carry, h == t)
        if h == t:
            acc = carry[1]
            o = acc[:LANES] / acc[LANES:LANES + 1]
            o = o[:, :tq] - lam * o[:, tq:]
            o = o * lax.rsqrt(jnp.mean(o * o, axis=0, keepdims=True) + RMS_EPS) * sg_ref[...]
            out_ref[0, 0, t * tq:(t + 1) * tq, :] = (o * (1.0 - lambda_init)).T.astype(BF16)
            carry = fresh
        if n + 1 < len(tasks):
            qs = qs_next


def _diff_flash(q, k, vt, lq1, lk1, lq2, lk2, subln_g, lambda_init, tq):
    B, T, C = q.shape
    H = C // LANES
    slopes = jnp.exp2(-8.0 * jnp.arange(1, H + 1, dtype=F32) / H) * LOG2_E
    assert T // FEAT_RADIX <= 256
    s1 = slopes.astype(BF16)
    r1 = slopes - s1.astype(F32)
    s2 = r1.astype(BF16)
    s3 = (r1 - s2.astype(F32)).astype(BF16)
    pieces = jnp.stack([s1, s2, s3], axis=1).astype(F32)
    qfeat = jnp.concatenate([pieces * FEAT_RADIX, pieces, jnp.zeros((H, LANES - 6), F32)], axis=1)
    qfeat = jnp.broadcast_to(qfeat.astype(BF16)[:, :, None], (H, LANES, LANES))
    pos = jnp.arange(T)
    hi = jnp.broadcast_to((pos // FEAT_RADIX)[:, None], (T, 3))
    lo = jnp.broadcast_to((pos % FEAT_RADIX)[:, None], (T, 3))
    kfeat = jnp.concatenate([hi, lo, jnp.zeros((T, LANES - 6), jnp.int32)], axis=1).astype(BF16)
    lvec = _const_spec((1, HEAD_DIM))
    seq = pl.BlockSpec((1, T, LANES), lambda b, h: (b, 0, h))
    return pl.pallas_call(
        functools.partial(_diff_flash_kernel, lambda_init=lambda_init, tq=tq),
        grid=(B, H),
        in_specs=[seq, seq, pl.BlockSpec((1, LANES, T), lambda b, h: (b, h, 0)),
                  pl.BlockSpec((1, LANES, LANES), lambda b, h: (h, 0, 0)), _const_spec((T, LANES)),
                  lvec, lvec, lvec, lvec, _const_spec((LANES, 1))],
        out_specs=pl.BlockSpec((1, 1, T, LANES), lambda b, h: (b, h, 0, 0)),
        out_shape=jax.ShapeDtypeStruct((B, H, T, LANES), BF16),
        scratch_shapes=[pltpu.VMEM((tq, 2 * tq), F32)] * 2,
        compiler_params=_params("arbitrary", "arbitrary"),
        name="diff_flash",
    )(q, k, vt, qfeat, kfeat,
      lq1.reshape(1, -1), lk1.reshape(1, -1), lq2.reshape(1, -1), lk2.reshape(1, -1),
      subln_g.reshape(LANES, 1))


def _attn_out_kernel(mix_ref, qm_ref, km_ref, vm_ref, x_ref, qg_ref, kg_ref, w_ref, gmean_ref, out_ref):
    qm = qm_ref[0].astype(F32)
    ms = _dot((qm * qm).astype(BF16), gmean_ref[...])
    qn = qm * lax.rsqrt(ms + RMS_EPS) * qg_ref[...]
    kb = (km_ref[0] * kg_ref[...]).astype(BF16)
    vb = vm_ref[0].astype(BF16)
    lane = lax.broadcasted_iota(jnp.int32, qn.shape, 1)
    mem = jnp.zeros_like(qn)
    for hd in range(MEM_HEADS):
        in_head = (lane >= hd * HEAD_DIM) & (lane < (hd + 1) * HEAD_DIM)
        s = _dot_nt(jnp.where(in_head, qn, 0.0).astype(BF16), kb)
        p = jnp.exp(s - jnp.max(s, axis=-1, keepdims=True))
        o = _dot(p.astype(BF16), vb) / jnp.sum(p, axis=-1, keepdims=True)
        mem = jnp.where(in_head, o, mem)
    both = jnp.concatenate([mix_ref[0, hp] for hp in range(HEAD_PAIRS)] + [mem.astype(BF16)], axis=1)
    out_ref[0] = x_ref[0] + _dot(both, w_ref[...])


def _attn_out(mix, qm, k_mem, v_mem, x, q_g, k_g, w_out, tm):
    B, T, D = x.shape
    M = k_mem.shape[1]
    gmean = _group_matrix(MEM_WIDTH, HEAD_DIM, 1.0 / HEAD_DIM)
    qg = jnp.tile(q_g, MEM_HEADS).reshape(1, MEM_WIDTH) * HEAD_DIM ** -0.5
    kg = jnp.tile(k_g, MEM_HEADS).reshape(1, MEM_WIDTH)
    rows = lambda n: pl.BlockSpec((1, tm, n), lambda b, t: (b, t, 0))
    memspec = pl.BlockSpec((1, M, MEM_WIDTH), lambda b, t: (b, 0, 0))
    return pl.pallas_call(
        _attn_out_kernel,
        grid=(B, T // tm),
        in_specs=[_slab_spec(tm), rows(MEM_WIDTH), memspec, memspec, rows(D),
                  _const_spec((1, MEM_WIDTH)), _const_spec((1, MEM_WIDTH)),
                  _const_spec((D, D)),
                  _const_spec((MEM_WIDTH, MEM_WIDTH))],
        out_specs=rows(D),
        out_shape=jax.ShapeDtypeStruct((B, T, D), F32),
        compiler_params=_params("arbitrary", "arbitrary"),
        name="attn_out",
    )(mix, qm, k_mem, v_mem, x, qg, kg, w_out.astype(BF16), gmean)


def _ffn_kernel(x_ref, ng_ref, w1_ref, w2_ref, out_ref, *, ff_chunk):
    x = x_ref[0]
    h = _rms_rows(x, ng_ref[...]).astype(BF16)
    acc = x
    for c in range(0, w1_ref.shape[1], ff_chunk):
        u = jnp.maximum(_dot(h, w1_ref[:, c:c + ff_chunk]), 0.0)
        acc = acc + _dot((u * u).astype(BF16), w2_ref[c:c + ff_chunk, :])
    out_ref[0] = acc


def _ffn(x, norm_g, w1, w2, tm):
    B, T, D = x.shape
    F = w1.shape[1]
    rows = pl.BlockSpec((1, tm, D), lambda b, t: (b, t, 0))
    return pl.pallas_call(
        functools.partial(_ffn_kernel, ff_chunk=D),
        grid=(B, T // tm),
        in_specs=[rows, _const_spec((1, D)), _const_spec((D, F)), _const_spec((F, D))],
        out_specs=rows,
        out_shape=jax.ShapeDtypeStruct((B, T, D), F32),
        compiler_params=_params("arbitrary", "arbitrary"),
        name="ffn",
    )(x, norm_g.reshape(1, D), w1.astype(BF16), w2.astype(BF16))


def _diff_lambda_init(layer):
    return 0.8 - 0.6 * math.exp(-0.3 * layer)


def _row_tiles(seq):
    tiles = {"rwkv_in": 512, "rwkv_scan": 512, "diff_in": 1024, "diff_flash": 512, "attn_out": 1024, "ffn": 1024}
    tiles = {name: min(seq, rows) for name, rows in tiles.items()}
    assert all(seq % rows == 0 for rows in tiles.values())
    assert tiles["rwkv_in"] % GROUP_COLS == 0 and tiles["rwkv_scan"] % CHUNK == 0
    return tiles


def kernel(x, mem, norm_mix_g, norm_ffn_g, w_out, w_ff1, w_ff2, mem_norm_g, w_mem_kv, mem_q_norm_g, mem_k_norm_g, rw_in, rw_mu, rw_w0, rw_w2, rw_a0, rw_a2, rw_g2, rw_k_k, rw_k_a, rw_r_k, rw_lnx_g, rw_lnx_b, df_in, df_q_norm_g, df_k_norm_g, df_lq1, df_lk1, df_lq2, df_lk2, df_subln_g):
    tiles = _row_tiles(x.shape[1])
    depth = norm_mix_g.shape[0]
    k_mem, v_mem = _mem_kv(mem, mem_norm_g, w_mem_kv)
    for layer in range(depth):
        j = layer // 2
        if layer % 2 == 0:
            r, k, v, kn, a, cs, bonus, g, qm = _rwkv_in(
                x, norm_mix_g[layer], rw_in[j], rw_mu[j], rw_w0[j], rw_w2[j], rw_a0[j], rw_a2[j],
                rw_g2[j], rw_k_k[j], rw_k_a[j], rw_r_k[j], tiles["rwkv_in"])
            mix = _rwkv_scan(r, k, v, kn, a, cs, bonus, g, rw_lnx_g[j], rw_lnx_b[j], tiles["rwkv_scan"])
        else:
            q, k, v, qm = _diff_in(x, norm_mix_g[layer], df_in[j], df_q_norm_g[j], df_k_norm_g[j],
                                   tiles["diff_in"])
            mix = _diff_flash(q, k, v, df_lq1[j], df_lk1[j], df_lq2[j], df_lk2[j], df_subln_g[j],
                              _diff_lambda_init(layer), tiles["diff_flash"])
        x = _attn_out(mix, qm, k_mem, v_mem, x, mem_q_norm_g[layer], mem_k_norm_g[layer], w_out[layer],
                      tiles["attn_out"])
        x = _ffn(x, norm_ffn_g[layer], w_ff1[layer], w_ff2[layer], tiles["ffn"])
    return x
```
